```python
import math
import jax, jax.numpy as jnp
from jax import lax
import numpy as np

D_MODEL = 2048
BATCH = 4
SEQ = 4096
DEPTH = 1

CHUNK = 64
HEAD_DIM = 128
N_HEADS_FOX = 8
N_HEADS_DSA = 8
FOX_WIDTH = N_HEADS_FOX * HEAD_DIM
DSA_WIDTH = N_HEADS_DSA * HEAD_DIM
MIX_WIDTH = FOX_WIDTH + DSA_WIDTH
IDX_HEADS = 16
IDX_DIM = 64
IDX_TOPK_MAX = 256
FOX_Q_BLOCK = 128
DSA_Q_BLOCK = 64
REL_BUCKETS = 32
REL_MAX_DIST = 128
FOX_GATE_BIAS_MEAN = 3.0
N_EXPERTS = 32
EXPERT_TOPK = 4
D_EXPERT = 2048
SWIGLU_ALPHA = 1.702
SWIGLU_LIMIT = 7.0
EXPERT_ROW_BLOCK = 128
NORM_EPS = 1e-6

IN_PROJ_SIZES = (
    FOX_WIDTH, FOX_WIDTH, FOX_WIDTH, N_HEADS_FOX,
    DSA_WIDTH, DSA_WIDTH, DSA_WIDTH,
    IDX_HEADS * IDX_DIM, IDX_DIM, IDX_HEADS,
)
IN_PROJ_WIDTH = 3 * FOX_WIDTH + N_HEADS_FOX + 3 * DSA_WIDTH + IDX_HEADS * IDX_DIM + IDX_DIM + IDX_HEADS

kernel_name = "fox_dsa_hybrid_moe_block"


def _split_points():
    pts, acc = [], 0
    for s in IN_PROJ_SIZES[:-1]:
        acc += s
        pts.append(acc)
    return pts


def rms_norm(x, gain):
    xf = x.astype(jnp.float32)
    y = xf * lax.rsqrt(jnp.mean(xf * xf, axis=-1, keepdims=True) + NORM_EPS)
    return (y * gain.astype(jnp.float32)).astype(x.dtype)


def t5_bucket(rel):
    half = REL_BUCKETS // 2
    max_exact = half // 2
    ret = jnp.where(rel > 0, half, 0)
    n = jnp.abs(rel)
    nf = jnp.maximum(n, 1).astype(jnp.float32)
    large = max_exact + (jnp.log(nf / max_exact) / math.log(REL_MAX_DIST / max_exact)
                         * (half - max_exact)).astype(jnp.int32)
    large = jnp.minimum(large, half - 1)
    return ret + jnp.where(n < max_exact, n, large)


def fox_attention(q, k, v, f_logit):
    B, S, H, dh = q.shape
    c = jnp.cumsum(jax.nn.log_sigmoid(f_logit.astype(jnp.float32)), axis=1)
    c_k = c.transpose(0, 2, 1)[:, :, None, :]
    key_pos = jnp.arange(S)
    scale = dh ** -0.5

    def block(i):
        t0 = i * FOX_Q_BLOCK
        qb = lax.dynamic_slice_in_dim(q, t0, FOX_Q_BLOCK, axis=1)
        cq = lax.dynamic_slice_in_dim(c, t0, FOX_Q_BLOCK, axis=1)
        s = jnp.einsum('bqhd,bkhd->bhqk', qb, k).astype(jnp.float32) * scale
        s = s + cq.transpose(0, 2, 1)[..., None] - c_k
        qpos = t0 + jnp.arange(FOX_Q_BLOCK)
        causal = key_pos[None, :] <= qpos[:, None]
        s = jnp.where(causal, s, -jnp.inf)
        p = jax.nn.softmax(s, axis=-1).astype(v.dtype)
        return jnp.einsum('bhqk,bkhd->bqhd', p, v)

    out = lax.map(block, jnp.arange(S // FOX_Q_BLOCK))
    return out.transpose(1, 0, 2, 3, 4).reshape(B, S, H * dh)


def dsa_attention(q, k, v, q_idx, k_idx, w_idx, rel_bias):
    B, S, H, dh = q.shape
    topk = min(IDX_TOPK_MAX, S // 4)
    scale = dh ** -0.5
    idx_scale = (IDX_DIM ** -0.5) * (IDX_HEADS ** -0.5)
    key_pos = jnp.arange(S)
    b_ix = jnp.arange(B)[:, None, None]
    k_idx_f = k_idx.astype(jnp.float32)
    bias_tab = rel_bias.astype(jnp.float32)

    def block(i):
        t0 = i * DSA_Q_BLOCK
        qpos = t0 + jnp.arange(DSA_Q_BLOCK)
        limit = (qpos // CHUNK + 1) * CHUNK
        qi = lax.dynamic_slice_in_dim(q_idx, t0, DSA_Q_BLOCK, axis=1).astype(jnp.float32)
        wi = lax.dynamic_slice_in_dim(w_idx, t0, DSA_Q_BLOCK, axis=1).astype(jnp.float32)
        dots = jnp.einsum('bqhd,bkd->bqhk', qi, k_idx_f)
        score = jnp.einsum('bqh,bqhk->bqk', wi, jax.nn.relu(dots)) * idx_scale
        score = jnp.where(key_pos[None, None, :] < limit[None, :, None], score, -jnp.inf)
        _, sel = lax.top_k(score, topk)
        valid = sel < limit[None, :, None]
        kg = k[b_ix, sel]
        vg = v[b_ix, sel]
        qb = lax.dynamic_slice_in_dim(q, t0, DSA_Q_BLOCK, axis=1)
        s = jnp.einsum('bqhd,bqkhd->bqhk', qb, kg).astype(jnp.float32) * scale
        bias = bias_tab[t5_bucket(sel - qpos[None, :, None])]
        s = s + bias.transpose(0, 1, 3, 2)
        s = jnp.where(valid[:, :, None, :], s, -jnp.inf)
        p = jax.nn.softmax(s, axis=-1).astype(v.dtype)
        return jnp.einsum('bqhk,bqkhd->bqhd', p, vg)

    out = lax.map(block, jnp.arange(S // DSA_Q_BLOCK))
    return out.transpose(1, 0, 2, 3, 4).reshape(B, S, H * dh)


def clamped_swiglu(h):
    x_glu, x_lin = h[..., ::2], h[..., 1::2]
    x_glu = jnp.minimum(x_glu, SWIGLU_LIMIT)
    x_lin = jnp.clip(x_lin, -SWIGLU_LIMIT, SWIGLU_LIMIT)
    return x_glu * jax.nn.sigmoid(SWIGLU_ALPHA * x_glu) * (x_lin + 1)


def moe(h, router_w, router_b, w1, b1, w2, b2):
    B, S, D = h.shape
    T = B * S
    xt = h.reshape(T, D)
    logits = (xt @ router_w + router_b).astype(jnp.float32)
    top_val, top_idx = lax.top_k(logits, EXPERT_TOPK)
    gates = jax.nn.softmax(top_val, axis=-1).astype(h.dtype)
    TK = T * EXPERT_TOPK
    e_flat = top_idx.reshape(TK)
    tok_flat = jnp.repeat(jnp.arange(T, dtype=jnp.int32), EXPERT_TOPK)
    g_flat = gates.reshape(TK)
    order = jnp.argsort(e_flat)
    e_sorted, tok_sorted, g_sorted = e_flat[order], tok_flat[order], g_flat[order]
    counts = jnp.bincount(e_flat, length=N_EXPERTS)
    starts = jnp.cumsum(counts) - counts
    padded = ((counts + EXPERT_ROW_BLOCK - 1) // EXPERT_ROW_BLOCK) * EXPERT_ROW_BLOCK
    pad_end = jnp.cumsum(padded)
    pad_start = pad_end - padded
    dest = pad_start[e_sorted] + (jnp.arange(TK) - starts[e_sorted])
    P = ((TK + EXPERT_ROW_BLOCK - 1) // EXPERT_ROW_BLOCK) * EXPERT_ROW_BLOCK + N_EXPERTS * EXPERT_ROW_BLOCK
    n_blk = P // EXPERT_ROW_BLOCK
    row_tok = jnp.zeros((P,), jnp.int32).at[dest].set(tok_sorted)
    row_gate = jnp.zeros((P,), h.dtype).at[dest].set(g_sorted)
    blk_expert = jnp.clip(jnp.searchsorted(pad_end, jnp.arange(n_blk) * EXPERT_ROW_BLOCK, side='right'),
                          0, N_EXPERTS - 1)

    def expert_block(args):
        tok, e = args
        xb = xt[tok]
        hb = xb @ w1[e] + b1[e]
        return clamped_swiglu(hb) @ w2[e] + b2[e]

    yb = lax.map(expert_block, (row_tok.reshape(n_blk, EXPERT_ROW_BLOCK), blk_expert))
    y_rows = yb.reshape(P, D) * row_gate[:, None]
    y = jnp.zeros((T, D), h.dtype).at[row_tok].add(y_rows)
    return y.reshape(B, S, D)


def setup_inputs(seed: int = 0) -> dict:
    key = jax.random.key(seed)
    ks = jax.random.split(key, 20)
    f32 = jnp.float32
    L, D, E, F = DEPTH, D_MODEL, N_EXPERTS, D_EXPERT
    nrm = lambda k, shape, s: jax.random.normal(k, shape, f32) * s
    return {
        "x": nrm(ks[0], (BATCH, SEQ, D), 1.0),
        "attn_norm_g": 1.0 + nrm(ks[1], (L, D), 0.02),
        "w_in": nrm(ks[2], (L, D, IN_PROJ_WIDTH), D ** -0.5),
        "fox_gate_b": FOX_GATE_BIAS_MEAN + nrm(ks[3], (L, N_HEADS_FOX), 0.1),
        "fox_q_g": 1.0 + nrm(ks[4], (L, HEAD_DIM), 0.02),
        "fox_k_g": 1.0 + nrm(ks[5], (L, HEAD_DIM), 0.02),
        "dsa_q_g": 1.0 + nrm(ks[6], (L, HEAD_DIM), 0.02),
        "dsa_k_g": 1.0 + nrm(ks[7], (L, HEAD_DIM), 0.02),
        "idx_k_g": 1.0 + nrm(ks[8], (L, IDX_DIM), 0.02),
        "rel_bias": nrm(ks[9], (REL_BUCKETS, N_HEADS_DSA), 0.5),
        "w_out": nrm(ks[10], (L, MIX_WIDTH, D), MIX_WIDTH ** -0.5),
        "ffn_norm_g": 1.0 + nrm(ks[11], (L, D), 0.02),
        "router_w": nrm(ks[12], (L, D, E), D ** -0.5),
        "router_b": nrm(ks[13], (L, E), 0.01),
        "w1": nrm(ks[14], (L, E, D, 2 * F), D ** -0.5),
        "b1": nrm(ks[15], (L, E, 2 * F), 0.01),
        "w2": nrm(ks[16], (L, E, F, D), F ** -0.5),
        "b2": nrm(ks[17], (L, E, D), 0.01),
    }


def reference(x, attn_norm_g, w_in, fox_gate_b, fox_q_g, fox_k_g, dsa_q_g, dsa_k_g, idx_k_g,
              rel_bias, w_out, ffn_norm_g, router_w, router_b, w1, b1, w2, b2):
    B, S, _ = x.shape
    split_pts = _split_points()
    for l in range(DEPTH):
        h = rms_norm(x, attn_norm_g[l])
        proj = h @ w_in[l]
        fq, fk, fv, ff, dq, dk, dv, iq, ik, iw = jnp.split(proj, split_pts, axis=-1)
        fq = rms_norm(fq.reshape(B, S, N_HEADS_FOX, HEAD_DIM), fox_q_g[l])
        fk = rms_norm(fk.reshape(B, S, N_HEADS_FOX, HEAD_DIM), fox_k_g[l])
        fv = fv.reshape(B, S, N_HEADS_FOX, HEAD_DIM)
        o_fox = fox_attention(fq, fk, fv, ff + fox_gate_b[l])
        dq = rms_norm(dq.reshape(B, S, N_HEADS_DSA, HEAD_DIM), dsa_q_g[l])
        dk = rms_norm(dk.reshape(B, S, N_HEADS_DSA, HEAD_DIM), dsa_k_g[l])
        dv = dv.reshape(B, S, N_HEADS_DSA, HEAD_DIM)
        iq = iq.reshape(B, S, IDX_HEADS, IDX_DIM)
        ik = rms_norm(ik, idx_k_g[l])
        o_dsa = dsa_attention(dq, dk, dv, iq, ik, iw, rel_bias)
        x = x + jnp.concatenate([o_fox, o_dsa], axis=-1) @ w_out[l]
        x = x + moe(rms_norm(x, ffn_norm_g[l]), router_w[l], router_b[l], w1[l], b1[l], w2[l], b2[l])
    return x
```

```python
import functools
import math

import jax
import jax.numpy as jnp
import numpy as np
from jax import lax
from jax.experimental import pallas as pl
from jax.experimental.pallas import tpu as pltpu

HEAD_DIM = 128
N_HEADS = 8
HEAD_WIDTH = N_HEADS * HEAD_DIM
IDX_HEADS = 16
IDX_DIM = 64
NORM_EPS = 1e-6
FOX_SCALE = HEAD_DIM ** -0.5
NEG_BIG = -1e30

COL_FQ, COL_FK, COL_DQ, COL_DK, COL_FV, COL_DV, COL_IQ = range(7)
N_BIG_TILES = 7
N_NORM_TILES = 4
SMALL_W = 256

VMEM_LIMIT = 56 * 1024 * 1024


def _cparams(sem):
    return pltpu.CompilerParams(dimension_semantics=sem, vmem_limit_bytes=VMEM_LIMIT)


def _inproj_kernel(x_ref, g_ref, wb_ref, ws_ref, hg_ref, big_ref, small_ref, xn_ref):
    j = pl.program_id(1)

    @pl.when(j == 0)
    def _():
        x = x_ref[...]
        ms = jnp.mean(x * x, axis=-1, keepdims=True)
        xn = (x * lax.rsqrt(ms + NORM_EPS) * g_ref[...]).astype(jnp.bfloat16)
        xn_ref[...] = xn
        small_ref[...] = jnp.dot(xn, ws_ref[...], preferred_element_type=jnp.float32)

    acc = jnp.dot(xn_ref[...], wb_ref[...], preferred_element_type=jnp.float32)

    @pl.when(j < N_NORM_TILES)
    def _():
        hg = hg_ref[...]
        for h in range(N_HEADS):
            t = acc[:, h * HEAD_DIM:(h + 1) * HEAD_DIM]
            ms = jnp.mean(t * t, axis=-1, keepdims=True)
            big_ref[:, h * HEAD_DIM:(h + 1) * HEAD_DIM] = (
                t * lax.rsqrt(ms + NORM_EPS) * hg).astype(big_ref.dtype)

    @pl.when(j >= N_NORM_TILES)
    def _():
        big_ref[...] = acc.astype(big_ref.dtype)


def _inproj(x2, g, w_big, w_small, head_gains, tm=1024):
    T, D = x2.shape
    return pl.pallas_call(
        _inproj_kernel,
        grid=(T // tm, N_BIG_TILES),
        in_specs=[
            pl.BlockSpec((tm, D), lambda i, j: (i, 0)),
            pl.BlockSpec((1, D), lambda i, j: (0, 0)),
            pl.BlockSpec((D, HEAD_WIDTH), lambda i, j: (0, j)),
            pl.BlockSpec((D, SMALL_W), lambda i, j: (0, 0)),
            pl.BlockSpec((None, 1, HEAD_DIM), lambda i, j: (jnp.minimum(j, N_NORM_TILES - 1), 0, 0)),
        ],
        out_specs=[
            pl.BlockSpec((tm, HEAD_WIDTH), lambda i, j: (i, j)),
            pl.BlockSpec((tm, SMALL_W), lambda i, j: (i, 0)),
        ],
        out_shape=[
            jax.ShapeDtypeStruct((T, N_BIG_TILES * HEAD_WIDTH), jnp.bfloat16),
            jax.ShapeDtypeStruct((T, SMALL_W), jnp.float32),
        ],
        scratch_shapes=[pltpu.VMEM((tm, D), jnp.bfloat16)],
        compiler_params=_cparams(("parallel", "arbitrary")),
        name="inproj",
    )(x2, g, w_big, w_small, head_gains)


def _prep_kernel(small_ref, gb_ref, ikg_ref, ccol_ref, crow_ref, ike_ref, iko_ref, *, tk):
    S = small_ref.shape[0]
    z = small_ref[:, 128:256] + gb_ref[...]
    ls = jnp.minimum(z, 0.0) - jnp.log(1.0 + jnp.exp(-jnp.abs(z)))
    row = lax.broadcasted_iota(jnp.int32, (S, 128), 0)
    c = ls
    sh = 1
    while sh < S:
        c = c + jnp.where(row >= sh, pltpu.roll(c, sh, axis=0), 0.0)
        sh *= 2
    ccol_ref[...] = c
    crow_ref[...] = c.T[0:N_HEADS, :]
    ik = small_ref[:, 0:128]
    ms = jnp.sum(ik * ik, axis=-1, keepdims=True) * (1.0 / 128.0)
    ikn = ik * lax.rsqrt(ms + NORM_EPS) * ikg_ref[...]
    lane = lax.broadcasted_iota(jnp.int32, (S, 128), 1)
    ike = jnp.where(lane < IDX_DIM, ikn, 0.0)
    iko = jnp.where(lane >= IDX_DIM, ikn, 0.0)
    for cidx in range(S // tk):
        ike_ref[cidx] = ike[cidx * tk:(cidx + 1) * tk, :].T.astype(ike_ref.dtype)
        iko_ref[cidx] = iko[cidx * tk:(cidx + 1) * tk, :].T.astype(iko_ref.dtype)


def _prep(small3, gate_b128, ikg128, tk=512):
    B, S, _ = small3.shape
    nk = S // tk
    return pl.pallas_call(
        functools.partial(_prep_kernel, tk=tk),
        grid=(B,),
        in_specs=[
            pl.BlockSpec((None, S, SMALL_W), lambda b: (b, 0, 0)),
            pl.BlockSpec((1, 128), lambda b: (0, 0)),
            pl.BlockSpec((1, 128), lambda b: (0, 0)),
        ],
        out_specs=[
            pl.BlockSpec((None, S, 128), lambda b: (b, 0, 0)),
            pl.BlockSpec((None, N_HEADS, S), lambda b: (b, 0, 0)),
            pl.BlockSpec((None, nk, 128, tk), lambda b: (b, 0, 0, 0)),
            pl.BlockSpec((None, nk, 128, tk), lambda b: (b, 0, 0, 0)),
        ],
        out_shape=[
            jax.ShapeDtypeStruct((B, S, 128), jnp.float32),
            jax.ShapeDtypeStruct((B, N_HEADS, S), jnp.float32),
            jax.ShapeDtypeStruct((B, nk, 128, tk), jnp.bfloat16),
            jax.ShapeDtypeStruct((B, nk, 128, tk), jnp.bfloat16),
        ],
        compiler_params=_cparams(("parallel",)),
        name="prep",
    )(small3, gate_b128, ikg128)


def _fox_kernel(q_ref, k_ref, v_ref, ccol_ref, ck_ref, o_ref, acc_ref, *, tq):
    h = pl.program_id(1)
    i = pl.program_id(2)
    q = q_ref[...]
    lane = lax.broadcasted_iota(jnp.int32, (tq, 128), 1)
    cq = jnp.sum(jnp.where(lane == h, ccol_ref[...], 0.0), axis=-1, keepdims=True)

    def tile(j, m, l, masked):
        kj = k_ref[pl.ds(pl.multiple_of(j * tq, tq), tq), :]
        vj = v_ref[pl.ds(pl.multiple_of(j * tq, tq), tq), :]
        s = lax.dot_general(q, kj, (((1,), (1,)), ((), ())), preferred_element_type=jnp.float32)
        t = s * FOX_SCALE - ck_ref[j]
        if masked:
            r = lax.broadcasted_iota(jnp.int32, (tq, tq), 0)
            c = lax.broadcasted_iota(jnp.int32, (tq, tq), 1)
            t = jnp.where(c <= r, t, NEG_BIG)
        m_new = jnp.maximum(m, jnp.max(t, axis=-1, keepdims=True) + cq)
        p = jnp.exp(t - (m_new - cq))
        alpha = jnp.exp(m - m_new)
        l_new = alpha * l + jnp.sum(p, axis=-1, keepdims=True)
        acc_ref[...] = alpha * acc_ref[...] + jnp.dot(
            p.astype(jnp.bfloat16), vj, preferred_element_type=jnp.float32)
        return m_new, l_new

    acc_ref[...] = jnp.zeros_like(acc_ref)
    m0 = jnp.full((tq, 1), NEG_BIG, jnp.float32)
    l0 = jnp.zeros((tq, 1), jnp.float32)
    m, l = lax.fori_loop(0, i, lambda j, c: tile(j, c[0], c[1], False), (m0, l0))
    m, l = tile(i, m, l, True)
    o_ref[...] = (acc_ref[...] / l).astype(o_ref.dtype)


def _fox_attention(big3, ccol, ck5, tq=512):
    B, S, _ = big3.shape
    nq = S // tq
    hb = HEAD_WIDTH // HEAD_DIM
    return pl.pallas_call(
        functools.partial(_fox_kernel, tq=tq),
        grid=(B, N_HEADS, nq),
        in_specs=[
            pl.BlockSpec((None, tq, HEAD_DIM), lambda b, h, i: (b, i, COL_FQ * hb + h)),
            pl.BlockSpec((None, S, HEAD_DIM), lambda b, h, i: (b, 0, COL_FK * hb + h)),
            pl.BlockSpec((None, S, HEAD_DIM), lambda b, h, i: (b, 0, COL_FV * hb + h)),
            pl.BlockSpec((None, tq, 128), lambda b, h, i: (b, i, 0)),
            pl.BlockSpec((None, None, nq, 1, tq), lambda b, h, i: (b, h, 0, 0, 0)),
        ],
        out_specs=pl.BlockSpec((None, tq, HEAD_DIM), lambda b, h, i: (b, i, h)),
        out_shape=jax.ShapeDtypeStruct((B, S, HEAD_WIDTH), jnp.bfloat16),
        scratch_shapes=[pltpu.VMEM((tq, HEAD_DIM), jnp.float32)],
        compiler_params=_cparams(("parallel", "parallel", "arbitrary")),
        name="fox_attn",
    )(big3, big3, big3, ccol, ck5)


CHUNK = 64
IDX_TOPK_MAX = 256
IDX_SCALE = (IDX_DIM ** -0.5) * (IDX_HEADS ** -0.5)
INT_MIN = -2 ** 31


def _indexer_kernel(iq_ref, w_ref, ike_ref, iko_ref, mask_ref, key_ref, *, tq, tk, topk):
    i = pl.program_id(1)
    nc = key_ref.shape[0]
    q0 = i * tq
    nch = (q0 + tq + tk - 1) // tk
    rowpos = q0 + lax.broadcasted_iota(jnp.int32, (tq, 1), 0)
    limit = (rowpos // CHUNK + 1) * CHUNK
    wv = w_ref[...]
    wcols = [wv[:, N_HEADS + h:N_HEADS + h + 1] for h in range(IDX_HEADS)]
    lane_pos = lax.broadcasted_iota(jnp.int32, (tq, tk), 1)

    def score_chunk(c, carry):
        acc = jnp.zeros((tq, tk), jnp.float32)
        for p in range(IDX_HEADS // 2):
            qp = iq_ref[:, p * 128:(p + 1) * 128]
            de = jnp.dot(qp, ike_ref[c], preferred_element_type=jnp.float32)
            do = jnp.dot(qp, iko_ref[c], preferred_element_type=jnp.float32)
            acc = acc + wcols[2 * p] * jnp.maximum(de, 0.0) + wcols[2 * p + 1] * jnp.maximum(do, 0.0)
        sc = jnp.where(lane_pos + c * tk < limit, acc * IDX_SCALE, -jnp.inf)
        bits = pltpu.bitcast(sc, jnp.int32)
        key_ref[c] = bits ^ ((bits >> 31) & 0x7FFFFFFF)
        return carry

    lax.fori_loop(0, nch, score_chunk, 0)

    def search(it, thr):
        cand = thr + (jnp.int32(1) << (31 - it))

        def count_chunk(c, cnt):
            ge = jnp.where(key_ref[c] >= cand, 1, 0)
            for s in range(tk // 128):
                cnt = cnt + ge[:, s * 128:(s + 1) * 128]
            return cnt

        cnt = lax.fori_loop(0, nch, count_chunk, jnp.zeros((tq, 128), jnp.int32))
        total = jnp.sum(cnt, axis=-1, keepdims=True)
        return jnp.where(total >= topk, cand, thr)

    thr = lax.fori_loop(0, 32, search, jnp.full((tq, 1), INT_MIN, jnp.int32))

    def write_chunk(c, carry):
        sel = jnp.where(key_ref[c] >= thr,
                        jnp.where(lane_pos + c * tk < limit, 0.0, NEG_BIG), NEG_BIG)
        mask_ref[c] = sel.astype(mask_ref.dtype)
        return carry

    lax.fori_loop(0, nch, write_chunk, 0)

    def fill_chunk(c, carry):
        mask_ref[c] = jnp.full((tq, tk), NEG_BIG, mask_ref.dtype)
        return carry

    lax.fori_loop(nch, nc, fill_chunk, 0)


def _indexer_mask(big3, small3, ike, iko, topk, tq=256):
    B, S, _ = big3.shape
    nc, _, tk = ike.shape[1:]
    return pl.pallas_call(
        functools.partial(_indexer_kernel, tq=tq, tk=tk, topk=topk),
        grid=(B, S // tq),
        in_specs=[
            pl.BlockSpec((None, tq, HEAD_WIDTH), lambda b, i: (b, i, COL_IQ)),
            pl.BlockSpec((None, tq, 128), lambda b, i: (b, i, 1)),
            pl.BlockSpec((None, nc, 128, tk), lambda b, i: (b, 0, 0, 0)),
            pl.BlockSpec((None, nc, 128, tk), lambda b, i: (b, 0, 0, 0)),
        ],
        out_specs=pl.BlockSpec((None, nc, tq, tk), lambda b, i: (b, 0, i, 0)),
        out_shape=jax.ShapeDtypeStruct((B, nc, S, tk), jnp.bfloat16),
        scratch_shapes=[pltpu.VMEM((nc, tq, tk), jnp.int32)],
        compiler_params=_cparams(("parallel", "arbitrary")),
        name="indexer_mask",
    )(big3, small3, ike, iko)


REL_BUCKETS = 32
REL_MAX_DIST = 128
FAR_BUCKET = REL_BUCKETS // 2 - 1


def _t5_bucket(rel):
    half = REL_BUCKETS // 2
    max_exact = half // 2
    ret = jnp.where(rel > 0, half, 0)
    n = jnp.abs(rel)
    nf = jnp.maximum(n, 1).astype(jnp.float32)
    large = max_exact + (jnp.log(nf / max_exact) / math.log(REL_MAX_DIST / max_exact)
                         * (half - max_exact)).astype(jnp.int32)
    large = jnp.minimum(large, half - 1)
    return ret + jnp.where(n < max_exact, n, large)


def _bias_tile_kernel(tab_ref, bucket_ref, out_ref):
    h = pl.program_id(0)
    for d in range(2):
        bk = bucket_ref[d]
        acc = jnp.zeros(bk.shape, jnp.float32)
        for b in range(REL_BUCKETS):
            acc = jnp.where(bk == b, tab_ref[h, b], acc)
        out_ref[d] = acc


def _bias_tiles(rel_bias, tq):
    r = jnp.arange(tq, dtype=jnp.int32)
    rel = r[None, :] - r[:, None]
    buckets = jnp.stack([_t5_bucket(rel), _t5_bucket(rel - tq)])
    return pl.pallas_call(
        _bias_tile_kernel,
        grid=(N_HEADS,),
        in_specs=[
            pl.BlockSpec(memory_space=pltpu.SMEM),
            pl.BlockSpec((2, tq, tq), lambda h: (0, 0, 0)),
        ],
        out_specs=pl.BlockSpec((None, 2, tq, tq), lambda h: (h, 0, 0, 0)),
        out_shape=jax.ShapeDtypeStruct((N_HEADS, 2, tq, tq), jnp.float32),
        compiler_params=_cparams(("parallel",)),
        name="t5_bias_tiles",
    )(rel_bias.T.astype(jnp.float32), buckets)


def _dsa_kernel(tab_ref, q_ref, k_ref, v_ref, mask_ref, bias_ref, o_ref, acc_ref, *, tq):
    i = pl.program_id(1)
    h = pl.program_id(2)
    q = q_ref[...]
    far_bias = tab_ref[h, FAR_BUCKET]

    def tile(j, m, l, near):
        kj = k_ref[pl.ds(pl.multiple_of(j * tq, tq), tq), :]
        vj = v_ref[pl.ds(pl.multiple_of(j * tq, tq), tq), :]
        s = lax.dot_general(q, kj, (((1,), (1,)), ((), ())), preferred_element_type=jnp.float32)
        t = s * FOX_SCALE + mask_ref[j].astype(jnp.float32)
        if near is None:
            shift = far_bias
        else:
            t = t + bias_ref[near]
            shift = 0.0
        m_new = jnp.maximum(m, jnp.max(t, axis=-1, keepdims=True) + shift)
        p = jnp.exp(t - (m_new - shift))
        alpha = jnp.exp(m - m_new)
        l_new = alpha * l + jnp.sum(p, axis=-1, keepdims=True)
        acc_ref[...] = alpha * acc_ref[...] + jnp.dot(
            p.astype(jnp.bfloat16), vj, preferred_element_type=jnp.float32)
        return m_new, l_new

    acc_ref[...] = jnp.zeros_like(acc_ref)
    m0 = jnp.full((tq, 1), NEG_BIG, jnp.float32)
    l0 = jnp.zeros((tq, 1), jnp.float32)
    m, l = lax.fori_loop(0, i - 1, lambda j, c: tile(j, c[0], c[1], None), (m0, l0))
    m, l = lax.cond(i >= 1, lambda c: tile(i - 1, c[0], c[1], 1), lambda c: c, (m, l))
    m, l = tile(i, m, l, 0)
    o_ref[...] = (acc_ref[...] / l).astype(o_ref.dtype)


def _dsa_attention(big3, mask, bias_near, rel_bias, tq=512):
    B, S, _ = big3.shape
    nq = S // tq
    nc = mask.shape[1]
    assert mask.shape[3] == tq and tq >= REL_MAX_DIST and tq % CHUNK == 0
    hb = HEAD_WIDTH // HEAD_DIM
    return pl.pallas_call(
        functools.partial(_dsa_kernel, tq=tq),
        grid=(B, nq, N_HEADS),
        in_specs=[
            pl.BlockSpec(memory_space=pltpu.SMEM),
            pl.BlockSpec((None, tq, HEAD_DIM), lambda b, i, h: (b, i, COL_DQ * hb + h)),
            pl.BlockSpec((None, S, HEAD_DIM), lambda b, i, h: (b, 0, COL_DK * hb + h)),
            pl.BlockSpec((None, S, HEAD_DIM), lambda b, i, h: (b, 0, COL_DV * hb + h)),
            pl.BlockSpec((None, nc, tq, tq), lambda b, i, h: (b, 0, i, 0)),
            pl.BlockSpec((None, 2, tq, tq), lambda b, i, h: (h, 0, 0, 0)),
        ],
        out_specs=pl.BlockSpec((None, tq, HEAD_DIM), lambda b, i, h: (b, i, h)),
        out_shape=jax.ShapeDtypeStruct((B, S, HEAD_WIDTH), jnp.bfloat16),
        scratch_shapes=[pltpu.VMEM((tq, HEAD_DIM), jnp.float32)],
        compiler_params=_cparams(("parallel", "parallel", "arbitrary")),
        name="dsa_attn",
    )(rel_bias.T.astype(jnp.float32), big3, big3, big3, mask, bias_near)


N_EXPERTS = 32
EXPERT_TOPK = 4
SWIGLU_ALPHA = 1.702
SWIGLU_LIMIT = 7.0


def _outproj_router_kernel(of_ref, od_ref, x_ref, wof_ref, wod_ref, g_ref, rwh_ref, rwl_ref, rb_ref,
                           x1_ref, xn_ref, eidx_ref, rank_ref, gate_ref, cnt_ref, carry_ref, *, tm):
    i = pl.program_id(0)

    @pl.when(i == 0)
    def _():
        carry_ref[...] = jnp.zeros_like(carry_ref)

    x1 = (x_ref[...]
          + jnp.dot(of_ref[...], wof_ref[...], preferred_element_type=jnp.float32)
          + jnp.dot(od_ref[...], wod_ref[...], preferred_element_type=jnp.float32))
    x1_ref[...] = x1
    ms = jnp.mean(x1 * x1, axis=-1, keepdims=True)
    xn = x1 * lax.rsqrt(ms + NORM_EPS) * g_ref[...]
    xn_ref[...] = xn
    xh = xn.astype(jnp.bfloat16)
    xl = (xn - xh.astype(jnp.float32)).astype(jnp.bfloat16)
    nt = (((1,), (1,)), ((), ()))
    logits = (lax.dot_general(rwh_ref[...], xh, nt, preferred_element_type=jnp.float32)
              + lax.dot_general(rwh_ref[...], xl, nt, preferred_element_type=jnp.float32)
              + lax.dot_general(rwl_ref[...], xh, nt, preferred_element_type=jnp.float32)
              + rb_ref[...])
    eio = lax.broadcasted_iota(jnp.int32, (N_EXPERTS, tm), 0)
    work = logits
    vals, idxs = [], []
    multihot = jnp.zeros((N_EXPERTS, tm), jnp.float32)
    for _ in range(EXPERT_TOPK):
        mx = jnp.max(work, axis=0, keepdims=True)
        ix = jnp.min(jnp.where(work == mx, eio, N_EXPERTS), axis=0, keepdims=True)
        hit = eio == ix
        multihot = jnp.where(hit, 1.0, multihot)
        work = jnp.where(hit, -jnp.inf, work)
        vals.append(mx)
        idxs.append(ix)
    ex = [jnp.exp(v - vals[0]) for v in vals]
    den = ex[0] + ex[1] + ex[2] + ex[3]
    a = lax.broadcasted_iota(jnp.int32, (tm, tm), 0)
    b = lax.broadcasted_iota(jnp.int32, (tm, tm), 1)
    upper = jnp.where(a < b, 1.0, 0.0).astype(jnp.bfloat16)
    before = jnp.dot(multihot.astype(jnp.bfloat16), upper,
                     preferred_element_type=jnp.float32) + carry_ref[:, 0:1]
    for r in range(EXPERT_TOPK):
        eidx_ref[r:r + 1, :] = idxs[r]
        gate_ref[r:r + 1, :] = ex[r] / den
        rank_ref[r:r + 1, :] = jnp.sum(jnp.where(eio == idxs[r], before, 0.0),
                                       axis=0, keepdims=True).astype(jnp.int32)
    carry_ref[...] = carry_ref[...] + jnp.sum(multihot, axis=1, keepdims=True)
    cnt_ref[...] = carry_ref[...].astype(jnp.int32)


def _outproj_router(o_fox, o_dsa, x2, wo_f, wo_d, g, rw_hi, rw_lo, rb, tm=512):
    T, D = x2.shape
    hw = o_fox.shape[1]
    full = lambda shape: pl.BlockSpec(shape, lambda i: tuple(0 for _ in shape))
    return pl.pallas_call(
        functools.partial(_outproj_router_kernel, tm=tm),
        grid=(T // tm,),
        in_specs=[
            pl.BlockSpec((tm, hw), lambda i: (i, 0)),
            pl.BlockSpec((tm, hw), lambda i: (i, 0)),
            pl.BlockSpec((tm, D), lambda i: (i, 0)),
            full((hw, D)), full((hw, D)), full((1, D)),
            full((N_EXPERTS, D)), full((N_EXPERTS, D)), full((N_EXPERTS, 1)),
        ],
        out_specs=[
            pl.BlockSpec((tm, D), lambda i: (i, 0)),
            pl.BlockSpec((tm, D), lambda i: (i, 0)),
            pl.BlockSpec((EXPERT_TOPK, tm), lambda i: (0, i)),
            pl.BlockSpec((EXPERT_TOPK, tm), lambda i: (0, i)),
            pl.BlockSpec((EXPERT_TOPK, tm), lambda i: (0, i)),
            full((N_EXPERTS, 128)),
        ],
        out_shape=[
            jax.ShapeDtypeStruct((T, D), jnp.float32),
            jax.ShapeDtypeStruct((T, D), jnp.float32),
            jax.ShapeDtypeStruct((EXPERT_TOPK, T), jnp.int32),
            jax.ShapeDtypeStruct((EXPERT_TOPK, T), jnp.int32),
            jax.ShapeDtypeStruct((EXPERT_TOPK, T), jnp.float32),
            jax.ShapeDtypeStruct((N_EXPERTS, 128), jnp.int32),
        ],
        scratch_shapes=[pltpu.VMEM((N_EXPERTS, 128), jnp.float32)],
        compiler_params=_cparams(("arbitrary",)),
        name="outproj_router",
    )(o_fox, o_dsa, x2, wo_f, wo_d, g, rw_hi, rw_lo, rb)


def _dispatch_kernel(dest_ref, xn_ref, xg_in_ref, xg_ref, sem, *, tt):
    del xg_in_ref

    def row_copy(t, k):
        return pltpu.make_async_copy(xn_ref.at[pl.ds(t, 1)], xg_ref.at[pl.ds(dest_ref[k, t], 1)], sem)

    def issue(t, c):
        for k in range(EXPERT_TOPK):
            row_copy(t, k).start()
        return c

    def drain(t, c):
        for k in range(EXPERT_TOPK):
            row_copy(t, k).wait()
        return c

    lax.fori_loop(0, tt, issue, 0)
    lax.fori_loop(0, tt, drain, 0)


def _dispatch(dest, xn, n_rows, tt=256):
    T, D = xn.shape
    xg0 = jnp.zeros((n_rows, D), xn.dtype)
    return pl.pallas_call(
        functools.partial(_dispatch_kernel, tt=tt),
        grid=(T // tt,),
        in_specs=[
            pl.BlockSpec((EXPERT_TOPK, tt), lambda i: (0, i), memory_space=pltpu.SMEM),
            pl.BlockSpec((tt, D), lambda i: (i, 0)),
            pl.BlockSpec(memory_space=pl.ANY),
        ],
        out_specs=pl.BlockSpec(memory_space=pl.ANY),
        out_shape=jax.ShapeDtypeStruct((n_rows, D), xn.dtype),
        scratch_shapes=[pltpu.SemaphoreType.DMA(())],
        input_output_aliases={2: 0},
        compiler_params=_cparams(("arbitrary",)),
        name="moe_dispatch",
    )(dest, xn, xg0)


def _expert_kernel(te_ref, nu_ref, x_ref, w1g_ref, w1l_ref, b1g_ref, b1l_ref, w2_ref, b2_ref,
                   y_ref, xb_ref, acc_ref):
    i = pl.program_id(0)
    f = pl.program_id(1)
    nf = pl.num_programs(1)

    @pl.when(i < nu_ref[0])
    def _():
        @pl.when(f == 0)
        def _():
            xb_ref[...] = x_ref[...].astype(xb_ref.dtype)
            acc_ref[...] = jnp.zeros_like(acc_ref)

        xb = xb_ref[...]
        glu = jnp.dot(xb, w1g_ref[...], preferred_element_type=jnp.float32) + b1g_ref[...]
        lin = jnp.dot(xb, w1l_ref[...], preferred_element_type=jnp.float32) + b1l_ref[...]
        glu = jnp.minimum(glu, SWIGLU_LIMIT)
        lin = jnp.clip(lin, -SWIGLU_LIMIT, SWIGLU_LIMIT)
        act = glu * (1.0 / (1.0 + jnp.exp(-SWIGLU_ALPHA * glu))) * (lin + 1.0)
        acc_ref[...] += jnp.dot(act.astype(jnp.bfloat16), w2_ref[...],
                                preferred_element_type=jnp.float32)

        @pl.when(f == nf - 1)
        def _():
            y_ref[...] = acc_ref[...] + b2_ref[...]

    @pl.when((i >= nu_ref[0]) & (f == nf - 1))
    def _():
        y_ref[...] = jnp.zeros_like(y_ref)


def _experts(tile_expert, n_used, xg, w1g, w1l, b1g, b1l, w2b, b2, tme, tf=512):
    P, D = xg.shape
    F = w2b.shape[1]
    nf = F // tf
    row = lambda i, f, te, nu: (jnp.minimum(i, nu[0] - 1), 0)
    exp = lambda i, te, nu: te[jnp.minimum(i, nu[0] - 1)]
    fblk = lambda i, f, nu: jnp.where(i < nu[0], f, nf - 1)
    grid_spec = pltpu.PrefetchScalarGridSpec(
        num_scalar_prefetch=2,
        grid=(P // tme, nf),
        in_specs=[
            pl.BlockSpec((tme, D), row),
            pl.BlockSpec((None, D, tf), lambda i, f, te, nu: (exp(i, te, nu), 0, fblk(i, f, nu))),
            pl.BlockSpec((None, D, tf), lambda i, f, te, nu: (exp(i, te, nu), 0, fblk(i, f, nu))),
            pl.BlockSpec((None, 1, tf), lambda i, f, te, nu: (exp(i, te, nu), 0, fblk(i, f, nu))),
            pl.BlockSpec((None, 1, tf), lambda i, f, te, nu: (exp(i, te, nu), 0, fblk(i, f, nu))),
            pl.BlockSpec((None, tf, D), lambda i, f, te, nu: (exp(i, te, nu), fblk(i, f, nu), 0)),
            pl.BlockSpec((None, 1, D), lambda i, f, te, nu: (exp(i, te, nu), 0, 0)),
        ],
        out_specs=pl.BlockSpec((tme, D), lambda i, f, te, nu: (i, 0)),
        scratch_shapes=[pltpu.VMEM((tme, D), jnp.bfloat16), pltpu.VMEM((tme, D), jnp.float32)],
    )
    return pl.pallas_call(
        _expert_kernel,
        grid_spec=grid_spec,
        out_shape=jax.ShapeDtypeStruct((P, D), jnp.float32),
        compiler_params=_cparams(("arbitrary", "arbitrary")),
        name="moe_experts",
    )(tile_expert, n_used, xg, w1g, w1l, b1g, b1l, w2b, b2)


def _combine_kernel(dest_ref, x1_ref, gt_ref, y_ref, o_ref, buf_ref, sem, *, tt):
    def row_copy(t, k):
        return pltpu.make_async_copy(y_ref.at[pl.ds(dest_ref[k, t], 1)],
                                     buf_ref.at[k, pl.ds(t, 1)], sem)

    def issue(t, c):
        for k in range(EXPERT_TOPK):
            row_copy(t, k).start()
        return c

    def drain(t, c):
        for k in range(EXPERT_TOPK):
            row_copy(t, k).wait()
        return c

    lax.fori_loop(0, tt, issue, 0)
    lax.fori_loop(0, tt, drain, 0)
    gt = gt_ref[...]
    out = x1_ref[...]
    for k in range(EXPERT_TOPK):
        out = out + gt[:, k:k + 1] * buf_ref[k]
    o_ref[...] = out


def _combine(dest, x1, gates_t, yrows, tt=128):
    T, D = x1.shape
    return pl.pallas_call(
        functools.partial(_combine_kernel, tt=tt),
        grid=(T // tt,),
        in_specs=[
            pl.BlockSpec((EXPERT_TOPK, tt), lambda i: (0, i), memory_space=pltpu.SMEM),
            pl.BlockSpec((tt, D), lambda i: (i, 0)),
            pl.BlockSpec((tt, 8), lambda i: (i, 0)),
            pl.BlockSpec(memory_space=pl.ANY),
        ],
        out_specs=pl.BlockSpec((tt, D), lambda i: (i, 0)),
        out_shape=jax.ShapeDtypeStruct((T, D), x1.dtype),
        scratch_shapes=[pltpu.VMEM((EXPERT_TOPK, tt, D), jnp.float32), pltpu.SemaphoreType.DMA(())],
        compiler_params=_cparams(("arbitrary",)),
        name="moe_combine",
    )(dest, x1, gates_t, yrows)


def kernel_parts(x, attn_norm_g, w_in, fox_gate_b, fox_q_g, fox_k_g, dsa_q_g, dsa_k_g, idx_k_g,
                 rel_bias, w_out, ffn_norm_g, router_w, router_b, w1, b1, w2, b2):
    B, S, D = x.shape
    T = B * S
    l = 0
    f32 = jnp.float32
    wi = w_in[l]
    o = 0
    cols = {}
    for name, width in (("fq", HEAD_WIDTH), ("fk", HEAD_WIDTH), ("fv", HEAD_WIDTH), ("ff", N_HEADS),
                        ("dq", HEAD_WIDTH), ("dk", HEAD_WIDTH), ("dv", HEAD_WIDTH),
                        ("iq", IDX_HEADS * IDX_DIM), ("ik", IDX_DIM), ("iw", IDX_HEADS)):
        cols[name] = wi[:, o:o + width]
        o += width
    w_big = jnp.concatenate([cols[n] for n in ("fq", "fk", "dq", "dk", "fv", "dv", "iq")],
                            axis=1).astype(jnp.bfloat16)
    pad = jnp.zeros((D, SMALL_W - 2 * IDX_DIM - N_HEADS - IDX_HEADS), f32)
    w_small = jnp.concatenate([cols["ik"], cols["ik"], cols["ff"], cols["iw"], pad],
                              axis=1).astype(jnp.bfloat16)
    head_gains = jnp.stack([fox_q_g[l], fox_k_g[l], dsa_q_g[l], dsa_k_g[l]]).reshape(4, 1, HEAD_DIM)
    gate_b128 = jnp.zeros((1, 128), f32).at[0, :N_HEADS].set(fox_gate_b[l])
    ikg128 = jnp.concatenate([idx_k_g[l], idx_k_g[l]]).reshape(1, 128)

    x2 = x.reshape(T, D)
    big, small = _inproj(x2, attn_norm_g[l].reshape(1, D), w_big, w_small, head_gains)
    big3 = big.reshape(B, S, N_BIG_TILES * HEAD_WIDTH)
    small3 = small.reshape(B, S, SMALL_W)
    ccol, crow, ike, iko = _prep(small3, gate_b128, ikg128)
    tq = 512
    ck5 = crow.reshape(B, N_HEADS, S // tq, 1, tq)
    o_fox = _fox_attention(big3, ccol, ck5, tq=tq)
    topk = min(IDX_TOPK_MAX, S // 4)
    mask = _indexer_mask(big3, small3, ike, iko, topk)
    bias_near = _bias_tiles(rel_bias, tq)
    o_dsa = _dsa_attention(big3, mask, bias_near, rel_bias, tq=tq)
    bf16 = jnp.bfloat16
    wo = w_out[l].astype(bf16)
    rw = router_w[l].T
    rw_hi = rw.astype(bf16)
    rw_lo = (rw - rw_hi.astype(f32)).astype(bf16)
    x1, xn, eidx, rank, gates, cnt = _outproj_router(
        o_fox.reshape(T, HEAD_WIDTH), o_dsa.reshape(T, HEAD_WIDTH), x2, wo[:HEAD_WIDTH], wo[HEAD_WIDTH:],
        ffn_norm_g[l].reshape(1, D), rw_hi, rw_lo, router_b[l].reshape(N_EXPERTS, 1))
    tme = 512
    n_rows = T * EXPERT_TOPK + N_EXPERTS * tme
    n_tiles = n_rows // tme
    counts = cnt[:, 0]
    tiles_e = (counts + tme - 1) // tme
    tile_end = jnp.cumsum(tiles_e)
    pad_start = (tile_end - tiles_e) * tme
    dest = pad_start[eidx] + rank
    tile_expert = jnp.clip(jnp.searchsorted(tile_end, jnp.arange(n_tiles, dtype=jnp.int32), side="right"),
                           0, N_EXPERTS - 1).astype(jnp.int32)
    n_used = tile_end[-1:].astype(jnp.int32)
    xg = _dispatch(dest, xn, n_rows)
    F = w2.shape[2]
    w1g = w1[l][:, :, 0::2].astype(bf16)
    w1l = w1[l][:, :, 1::2].astype(bf16)
    b1g = b1[l][:, 0::2].reshape(N_EXPERTS, 1, F)
    b1l = b1[l][:, 1::2].reshape(N_EXPERTS, 1, F)
    yrows = _experts(tile_expert, n_used, xg, w1g, w1l, b1g, b1l, w2[l].astype(bf16),
                     b2[l].reshape(N_EXPERTS, 1, D), tme)
    gates_t = jnp.pad(gates.T, ((0, 0), (0, 8 - EXPERT_TOPK)))
    out = _combine(dest, x1, gates_t, yrows)
    return dict(o_fox=o_fox, o_dsa=o_dsa, mask=mask, x1=x1, out=out.reshape(B, S, D))


def kernel(x, attn_norm_g, w_in, fox_gate_b, fox_q_g, fox_k_g, dsa_q_g, dsa_k_g, idx_k_g,
           rel_bias, w_out, ffn_norm_g, router_w, router_b, w1, b1, w2, b2):
    return kernel_parts(x, attn_norm_g, w_in, fox_gate_b, fox_q_g, fox_k_g, dsa_q_g, dsa_k_g, idx_k_g,
                        rel_bias, w_out, ffn_norm_g, router_w, router_b, w1, b1, w2, b2)["out"]
```

```python
import functools
import math

import jax
import jax.numpy as jnp
import numpy as np
from jax import lax
from jax.experimental import pallas as pl
from jax.experimental.pallas import tpu as pltpu

HEAD_DIM = 128
N_HEADS = 8
HEAD_WIDTH = N_HEADS * HEAD_DIM
IDX_HEADS = 16
IDX_DIM = 64
NORM_EPS = 1e-6
FOX_SCALE = HEAD_DIM ** -0.5
NEG_BIG = -1e30

COL_FQ, COL_FK, COL_DQ, COL_DK, COL_FV, COL_DV, COL_IQ = range(7)
N_BIG_TILES = 7
N_NORM_TILES = 4
SMALL_W = 256

VMEM_LIMIT = 56 * 1024 * 1024


def _cparams(sem):
    return pltpu.CompilerParams(dimension_semantics=sem, vmem_limit_bytes=VMEM_LIMIT)


def _inproj_kernel(x_ref, g_ref, wb_ref, ws_ref, hg_ref, big_ref, small_ref, xn_ref):
    j = pl.program_id(1)

    @pl.when(j == 0)
    def _():
        x = x_ref[...]
        ms = jnp.mean(x * x, axis=-1, keepdims=True)
        xn = (x * lax.rsqrt(ms + NORM_EPS) * g_ref[...]).astype(jnp.bfloat16)
        xn_ref[...] = xn
        small_ref[...] = jnp.dot(xn, ws_ref[...], preferred_element_type=jnp.float32)

    acc = jnp.dot(xn_ref[...], wb_ref[...], preferred_element_type=jnp.float32)

    @pl.when(j < N_NORM_TILES)
    def _():
        hg = hg_ref[...]
        for h in range(N_HEADS):
            t = acc[:, h * HEAD_DIM:(h + 1) * HEAD_DIM]
            ms = jnp.mean(t * t, axis=-1, keepdims=True)
            big_ref[:, h * HEAD_DIM:(h + 1) * HEAD_DIM] = (
                t * lax.rsqrt(ms + NORM_EPS) * hg).astype(big_ref.dtype)

    @pl.when(j >= N_NORM_TILES)
    def _():
        big_ref[...] = acc.astype(big_ref.dtype)


def _inproj(x2, g, w_big, w_small, head_gains, tm=1024):
    T, D = x2.shape
    return pl.pallas_call(
        _inproj_kernel,
        grid=(T // tm, N_BIG_TILES),
        in_specs=[
            pl.BlockSpec((tm, D), lambda i, j: (i, 0)),
            pl.BlockSpec((1, D), lambda i, j: (0, 0)),
            pl.BlockSpec((D, HEAD_WIDTH), lambda i, j: (0, j)),
            pl.BlockSpec((D, SMALL_W), lambda i, j: (0, 0)),
            pl.BlockSpec((None, 1, HEAD_DIM), lambda i, j: (jnp.minimum(j, N_NORM_TILES - 1), 0, 0)),
        ],
        out_specs=[
            pl.BlockSpec((tm, HEAD_WIDTH), lambda i, j: (i, j)),
            pl.BlockSpec((tm, SMALL_W), lambda i, j: (i, 0)),
        ],
        out_shape=[
            jax.ShapeDtypeStruct((T, N_BIG_TILES * HEAD_WIDTH), jnp.bfloat16),
            jax.ShapeDtypeStruct((T, SMALL_W), jnp.float32),
        ],
        scratch_shapes=[pltpu.VMEM((tm, D), jnp.bfloat16)],
        compiler_params=_cparams(("parallel", "arbitrary")),
        name="inproj",
    )(x2, g, w_big, w_small, head_gains)


def _prep_kernel(small_ref, gb_ref, ikg_ref, ccol_ref, crow_ref, ike_ref, iko_ref, *, tk):
    S = small_ref.shape[0]
    z = small_ref[:, 128:256] + gb_ref[...]
    ls = jnp.minimum(z, 0.0) - jnp.log(1.0 + jnp.exp(-jnp.abs(z)))
    row = lax.broadcasted_iota(jnp.int32, (S, 128), 0)
    c = ls
    sh = 1
    while sh < S:
        c = c + jnp.where(row >= sh, pltpu.roll(c, sh, axis=0), 0.0)
        sh *= 2
    ccol_ref[...] = c
    crow_ref[...] = c.T[0:N_HEADS, :]
    ik = small_ref[:, 0:128]
    ms = jnp.sum(ik * ik, axis=-1, keepdims=True) * (1.0 / 128.0)
    ikn = ik * lax.rsqrt(ms + NORM_EPS) * ikg_ref[...]
    lane = lax.broadcasted_iota(jnp.int32, (S, 128), 1)
    ike = jnp.where(lane < IDX_DIM, ikn, 0.0)
    iko = jnp.where(lane >= IDX_DIM, ikn, 0.0)
    for cidx in range(S // tk):
        ike_ref[cidx] = ike[cidx * tk:(cidx + 1) * tk, :].T.astype(ike_ref.dtype)
        iko_ref[cidx] = iko[cidx * tk:(cidx + 1) * tk, :].T.astype(iko_ref.dtype)


def _prep(small3, gate_b128, ikg128, tk=512):
    B, S, _ = small3.shape
    nk = S // tk
    return pl.pallas_call(
        functools.partial(_prep_kernel, tk=tk),
        grid=(B,),
        in_specs=[
            pl.BlockSpec((None, S, SMALL_W), lambda b: (b, 0, 0)),
            pl.BlockSpec((1, 128), lambda b: (0, 0)),
            pl.BlockSpec((1, 128), lambda b: (0, 0)),
        ],
        out_specs=[
            pl.BlockSpec((None, S, 128), lambda b: (b, 0, 0)),
            pl.BlockSpec((None, N_HEADS, S), lambda b: (b, 0, 0)),
            pl.BlockSpec((None, nk, 128, tk), lambda b: (b, 0, 0, 0)),
            pl.BlockSpec((None, nk, 128, tk), lambda b: (b, 0, 0, 0)),
        ],
        out_shape=[
            jax.ShapeDtypeStruct((B, S, 128), jnp.float32),
            jax.ShapeDtypeStruct((B, N_HEADS, S), jnp.float32),
            jax.ShapeDtypeStruct((B, nk, 128, tk), jnp.bfloat16),
            jax.ShapeDtypeStruct((B, nk, 128, tk), jnp.bfloat16),
        ],
        compiler_params=_cparams(("parallel",)),
        name="prep",
    )(small3, gate_b128, ikg128)


def _fox_kernel(q_ref, k_ref, v_ref, ccol_ref, ck_ref, o_ref, acc_ref, *, tq):
    h = pl.program_id(1)
    i = pl.program_id(2)
    q = q_ref[...]
    lane = lax.broadcasted_iota(jnp.int32, (tq, 128), 1)
    cq = jnp.sum(jnp.where(lane == h, ccol_ref[...], 0.0), axis=-1, keepdims=True)

    def tile(j, m, l, masked):
        kj = k_ref[pl.ds(pl.multiple_of(j * tq, tq), tq), :]
        vj = v_ref[pl.ds(pl.multiple_of(j * tq, tq), tq), :]
        s = lax.dot_general(q, kj, (((1,), (1,)), ((), ())), preferred_element_type=jnp.float32)
        t = s * FOX_SCALE - ck_ref[j]
        if masked:
            r = lax.broadcasted_iota(jnp.int32, (tq, tq), 0)
            c = lax.broadcasted_iota(jnp.int32, (tq, tq), 1)
            t = jnp.where(c <= r, t, NEG_BIG)
        m_new = jnp.maximum(m, jnp.max(t, axis=-1, keepdims=True) + cq)
        p = jnp.exp(t - (m_new - cq))
        alpha = jnp.exp(m - m_new)
        l_new = alpha * l + jnp.sum(p, axis=-1, keepdims=True)
        acc_ref[...] = alpha * acc_ref[...] + jnp.dot(
            p.astype(jnp.bfloat16), vj, preferred_element_type=jnp.float32)
        return m_new, l_new

    acc_ref[...] = jnp.zeros_like(acc_ref)
    m0 = jnp.full((tq, 1), NEG_BIG, jnp.float32)
    l0 = jnp.zeros((tq, 1), jnp.float32)
    m, l = lax.fori_loop(0, i, lambda j, c: tile(j, c[0], c[1], False), (m0, l0))
    m, l = tile(i, m, l, True)
    o_ref[...] = (acc_ref[...] / l).astype(o_ref.dtype)


def _fox_attention(big3, ccol, ck5, tq=512):
    B, S, _ = big3.shape
    nq = S // tq
    hb = HEAD_WIDTH // HEAD_DIM
    return pl.pallas_call(
        functools.partial(_fox_kernel, tq=tq),
        grid=(B, N_HEADS, nq),
        in_specs=[
            pl.BlockSpec((None, tq, HEAD_DIM), lambda b, h, i: (b, i, COL_FQ * hb + h)),
            pl.BlockSpec((None, S, HEAD_DIM), lambda b, h, i: (b, 0, COL_FK * hb + h)),
            pl.BlockSpec((None, S, HEAD_DIM), lambda b, h, i: (b, 0, COL_FV * hb + h)),
            pl.BlockSpec((None, tq, 128), lambda b, h, i: (b, i, 0)),
            pl.BlockSpec((None, None, nq, 1, tq), lambda b, h, i: (b, h, 0, 0, 0)),
        ],
        out_specs=pl.BlockSpec((None, tq, HEAD_DIM), lambda b, h, i: (b, i, h)),
        out_shape=jax.ShapeDtypeStruct((B, S, HEAD_WIDTH), jnp.bfloat16),
        scratch_shapes=[pltpu.VMEM((tq, HEAD_DIM), jnp.float32)],
        compiler_params=_cparams(("parallel", "parallel", "arbitrary")),
        name="fox_attn",
    )(big3, big3, big3, ccol, ck5)


CHUNK = 64
IDX_TOPK_MAX = 256
IDX_SCALE = (IDX_DIM ** -0.5) * (IDX_HEADS ** -0.5)
INT_MIN = -2 ** 31


def _indexer_kernel(iq_ref, w_ref, ike_ref, iko_ref, mask_ref, key_ref, *, tq, tk, topk):
    i = pl.program_id(1)
    nc = key_ref.shape[0]
    q0 = i * tq
    nch = (q0 + tq + tk - 1) // tk
    rowpos = q0 + lax.broadcasted_iota(jnp.int32, (tq, 1), 0)
    limit = (rowpos // CHUNK + 1) * CHUNK
    wv = w_ref[...]
    wcols = [wv[:, N_HEADS + h:N_HEADS + h + 1] for h in range(IDX_HEADS)]
    lane_pos = lax.broadcasted_iota(jnp.int32, (tq, tk), 1)

    def score_chunk(c, carry):
        acc = jnp.zeros((tq, tk), jnp.float32)
        for p in range(IDX_HEADS // 2):
            qp = iq_ref[:, p * 128:(p + 1) * 128]
            de = jnp.dot(qp, ike_ref[c], preferred_element_type=jnp.float32)
            do = jnp.dot(qp, iko_ref[c], preferred_element_type=jnp.float32)
            acc = acc + wcols[2 * p] * jnp.maximum(de, 0.0) + wcols[2 * p + 1] * jnp.maximum(do, 0.0)
        sc = jnp.where(lane_pos + c * tk < limit, acc * IDX_SCALE, -jnp.inf)
        bits = pltpu.bitcast(sc, jnp.int32)
        key_ref[c] = bits ^ ((bits >> 31) & 0x7FFFFFFF)
        return carry

    lax.fori_loop(0, nch, score_chunk, 0)

    def search(it, thr):
        cand = thr + (jnp.int32(1) << (31 - it))

        def count_chunk(c, cnt):
            ge = jnp.where(key_ref[c] >= cand, 1, 0)
            for s in range(tk // 128):
                cnt = cnt + ge[:, s * 128:(s + 1) * 128]
            return cnt

        cnt = lax.fori_loop(0, nch, count_chunk, jnp.zeros((tq, 128), jnp.int32))
        total = jnp.sum(cnt, axis=-1, keepdims=True)
        return jnp.where(total >= topk, cand, thr)

    thr = lax.fori_loop(0, 32, search, jnp.full((tq, 1), INT_MIN, jnp.int32))

    def write_chunk(c, carry):
        sel = jnp.where(key_ref[c] >= thr,
                        jnp.where(lane_pos + c * tk < limit, 0.0, NEG_BIG), NEG_BIG)
        mask_ref[c] = sel.astype(mask_ref.dtype)
        return carry

    lax.fori_loop(0, nch, write_chunk, 0)

    def fill_chunk(c, carry):
        mask_ref[c] = jnp.full((tq, tk), NEG_BIG, mask_ref.dtype)
        return carry

    lax.fori_loop(nch, nc, fill_chunk, 0)


def _indexer_mask(big3, small3, ike, iko, topk, tq=256):
    B, S, _ = big3.shape
    nc, _, tk = ike.shape[1:]
    return pl.pallas_call(
        functools.partial(_indexer_kernel, tq=tq, tk=tk, topk=topk),
        grid=(B, S // tq),
        in_specs=[
            pl.BlockSpec((None, tq, HEAD_WIDTH), lambda b, i: (b, i, COL_IQ)),
            pl.BlockSpec((None, tq, 128), lambda b, i: (b, i, 1)),
            pl.BlockSpec((None, nc, 128, tk), lambda b, i: (b, 0, 0, 0)),
            pl.BlockSpec((None, nc, 128, tk), lambda b, i: (b, 0, 0, 0)),
        ],
        out_specs=pl.BlockSpec((None, nc, tq, tk), lambda b, i: (b, 0, i, 0)),
        out_shape=jax.ShapeDtypeStruct((B, nc, S, tk), jnp.bfloat16),
        scratch_shapes=[pltpu.VMEM((nc, tq, tk), jnp.int32)],
        compiler_params=_cparams(("parallel", "arbitrary")),
        name="indexer_mask",
    )(big3, small3, ike, iko)


REL_BUCKETS = 32
REL_MAX_DIST = 128
FAR_BUCKET = REL_BUCKETS // 2 - 1


def _t5_bucket(rel):
    half = REL_BUCKETS // 2
    max_exact = half // 2
    ret = jnp.where(rel > 0, half, 0)
    n = jnp.abs(rel)
    nf = jnp.maximum(n, 1).astype(jnp.float32)
    large = max_exact + (jnp.log(nf / max_exact) / math.log(REL_MAX_DIST / max_exact)
                         * (half - max_exact)).astype(jnp.int32)
    large = jnp.minimum(large, half - 1)
    return ret + jnp.where(n < max_exact, n, large)


def _bias_tile_kernel(tab_ref, bucket_ref, out_ref):
    h = pl.program_id(0)
    for d in range(2):
        bk = bucket_ref[d]
        acc = jnp.zeros(bk.shape, jnp.float32)
        for b in range(REL_BUCKETS):
            acc = jnp.where(bk == b, tab_ref[h, b], acc)
        out_ref[d] = acc


def _bias_tiles(rel_bias, tq):
    r = jnp.arange(tq, dtype=jnp.int32)
    rel = r[None, :] - r[:, None]
    buckets = jnp.stack([_t5_bucket(rel), _t5_bucket(rel - tq)])
    return pl.pallas_call(
        _bias_tile_kernel,
        grid=(N_HEADS,),
        in_specs=[
            pl.BlockSpec(memory_space=pltpu.SMEM),
            pl.BlockSpec((2, tq, tq), lambda h: (0, 0, 0)),
        ],
        out_specs=pl.BlockSpec((None, 2, tq, tq), lambda h: (h, 0, 0, 0)),
        out_shape=jax.ShapeDtypeStruct((N_HEADS, 2, tq, tq), jnp.float32),
        compiler_params=_cparams(("parallel",)),
        name="t5_bias_tiles",
    )(rel_bias.T.astype(jnp.float32), buckets)


def _dsa_kernel(tab_ref, q_ref, k_ref, v_ref, mask_ref, bias_ref, o_ref, acc_ref, *, tq):
    i = pl.program_id(1)
    h = pl.program_id(2)
    q = q_ref[...]
    far_bias = tab_ref[h, FAR_BUCKET]

    def tile(j, m, l, near):
        kj = k_ref[pl.ds(pl.multiple_of(j * tq, tq), tq), :]
        vj = v_ref[pl.ds(pl.multiple_of(j * tq, tq), tq), :]
        s = lax.dot_general(q, kj, (((1,), (1,)), ((), ())), preferred_element_type=jnp.float32)
        t = s * FOX_SCALE + mask_ref[j].astype(jnp.float32)
        if near is None:
            shift = far_bias
        else:
            t = t + bias_ref[near]
            shift = 0.0
        m_new = jnp.maximum(m, jnp.max(t, axis=-1, keepdims=True) + shift)
        p = jnp.exp(t - (m_new - shift))
        alpha = jnp.exp(m - m_new)
        l_new = alpha * l + jnp.sum(p, axis=-1, keepdims=True)
        acc_ref[...] = alpha * acc_ref[...] + jnp.dot(
            p.astype(jnp.bfloat16), vj, preferred_element_type=jnp.float32)
        return m_new, l_new

    acc_ref[...] = jnp.zeros_like(acc_ref)
    m0 = jnp.full((tq, 1), NEG_BIG, jnp.float32)
    l0 = jnp.zeros((tq, 1), jnp.float32)
    m, l = lax.fori_loop(0, i - 1, lambda j, c: tile(j, c[0], c[1], None), (m0, l0))
    m, l = lax.cond(i >= 1, lambda c: tile(i - 1, c[0], c[1], 1), lambda c: c, (m, l))
    m, l = tile(i, m, l, 0)
    o_ref[...] = (acc_ref[...] / l).astype(o_ref.dtype)


def _dsa_attention(big3, mask, bias_near, rel_bias, tq=512):
    B, S, _ = big3.shape
    nq = S // tq
    nc = mask.shape[1]
    assert mask.shape[3] == tq and tq >= REL_MAX_DIST and tq % CHUNK == 0
    hb = HEAD_WIDTH // HEAD_DIM
    return pl.pallas_call(
        functools.partial(_dsa_kernel, tq=tq),
        grid=(B, nq, N_HEADS),
        in_specs=[
            pl.BlockSpec(memory_space=pltpu.SMEM),
            pl.BlockSpec((None, tq, HEAD_DIM), lambda b, i, h: (b, i, COL_DQ * hb + h)),
            pl.BlockSpec((None, S, HEAD_DIM), lambda b, i, h: (b, 0, COL_DK * hb + h)),
            pl.BlockSpec((None, S, HEAD_DIM), lambda b, i, h: (b, 0, COL_DV * hb + h)),
            pl.BlockSpec((None, nc, tq, tq), lambda b, i, h: (b, 0, i, 0)),
            pl.BlockSpec((None, 2, tq, tq), lambda b, i, h: (h, 0, 0, 0)),
        ],
        out_specs=pl.BlockSpec((None, tq, HEAD_DIM), lambda b, i, h: (b, i, h)),
        out_shape=jax.ShapeDtypeStruct((B, S, HEAD_WIDTH), jnp.bfloat16),
        scratch_shapes=[pltpu.VMEM((tq, HEAD_DIM), jnp.float32)],
        compiler_params=_cparams(("parallel", "parallel", "arbitrary")),
        name="dsa_attn",
    )(rel_bias.T.astype(jnp.float32), big3, big3, big3, mask, bias_near)


N_EXPERTS = 32
EXPERT_TOPK = 4
SWIGLU_ALPHA = 1.702
SWIGLU_LIMIT = 7.0


def _outproj_router_kernel(of_ref, od_ref, x_ref, wof_ref, wod_ref, g_ref, rwh_ref, rwl_ref, rb_ref,
                           x1_ref, xn_ref, eidx_ref, rank_ref, gate_ref, cnt_ref, carry_ref, *, tm):
    i = pl.program_id(0)

    @pl.when(i == 0)
    def _():
        carry_ref[...] = jnp.zeros_like(carry_ref)

    x1 = (x_ref[...]
          + jnp.dot(of_ref[...], wof_ref[...], preferred_element_type=jnp.float32)
          + jnp.dot(od_ref[...], wod_ref[...], preferred_element_type=jnp.float32))
    x1_ref[...] = x1
    ms = jnp.mean(x1 * x1, axis=-1, keepdims=True)
    xn = x1 * lax.rsqrt(ms + NORM_EPS) * g_ref[...]
    xn_ref[...] = xn
    xh = xn.astype(jnp.bfloat16)
    xl = (xn - xh.astype(jnp.float32)).astype(jnp.bfloat16)
    nt = (((1,), (1,)), ((), ()))
    logits = (lax.dot_general(rwh_ref[...], xh, nt, preferred_element_type=jnp.float32)
              + lax.dot_general(rwh_ref[...], xl, nt, preferred_element_type=jnp.float32)
              + lax.dot_general(rwl_ref[...], xh, nt, preferred_element_type=jnp.float32)
              + rb_ref[...])
    eio = lax.broadcasted_iota(jnp.int32, (N_EXPERTS, tm), 0)
    work = logits
    vals, idxs = [], []
    multihot = jnp.zeros((N_EXPERTS, tm), jnp.float32)
    for _ in range(EXPERT_TOPK):
        mx = jnp.max(work, axis=0, keepdims=True)
        ix = jnp.min(jnp.where(work == mx, eio, N_EXPERTS), axis=0, keepdims=True)
        hit = eio == ix
        multihot = jnp.where(hit, 1.0, multihot)
        work = jnp.where(hit, -jnp.inf, work)
        vals.append(mx)
        idxs.append(ix)
    ex = [jnp.exp(v - vals[0]) for v in vals]
    den = ex[0] + ex[1] + ex[2] + ex[3]
    a = lax.broadcasted_iota(jnp.int32, (tm, tm), 0)
    b = lax.broadcasted_iota(jnp.int32, (tm, tm), 1)
    upper = jnp.where(a < b, 1.0, 0.0).astype(jnp.bfloat16)
    before = jnp.dot(multihot.astype(jnp.bfloat16), upper,
                     preferred_element_type=jnp.float32) + carry_ref[:, 0:1]
    for r in range(EXPERT_TOPK):
        eidx_ref[r:r + 1, :] = idxs[r]
        gate_ref[r:r + 1, :] = ex[r] / den
        rank_ref[r:r + 1, :] = jnp.sum(jnp.where(eio == idxs[r], before, 0.0),
                                       axis=0, keepdims=True).astype(jnp.int32)
    carry_ref[...] = carry_ref[...] + jnp.sum(multihot, axis=1, keepdims=True)
    cnt_ref[...] = carry_ref[...].astype(jnp.int32)


def _outproj_router(o_fox, o_dsa, x2, wo_f, wo_d, g, rw_hi, rw_lo, rb, tm=512):
    T, D = x2.shape
    hw = o_fox.shape[1]
    full = lambda shape: pl.BlockSpec(shape, lambda i: tuple(0 for _ in shape))
    return pl.pallas_call(
        functools.partial(_outproj_router_kernel, tm=tm),
        grid=(T // tm,),
        in_specs=[
            pl.BlockSpec((tm, hw), lambda i: (i, 0)),
            pl.BlockSpec((tm, hw), lambda i: (i, 0)),
            pl.BlockSpec((tm, D), lambda i: (i, 0)),
            full((hw, D)), full((hw, D)), full((1, D)),
            full((N_EXPERTS, D)), full((N_EXPERTS, D)), full((N_EXPERTS, 1)),
        ],
        out_specs=[
            pl.BlockSpec((tm, D), lambda i: (i, 0)),
            pl.BlockSpec((tm, D), lambda i: (i, 0)),
            pl.BlockSpec((EXPERT_TOPK, tm), lambda i: (0, i)),
            pl.BlockSpec((EXPERT_TOPK, tm), lambda i: (0, i)),
            pl.BlockSpec((EXPERT_TOPK, tm), lambda i: (0, i)),
            full((N_EXPERTS, 128)),
        ],
        out_shape=[
            jax.ShapeDtypeStruct((T, D), jnp.float32),
            jax.ShapeDtypeStruct((T, D), jnp.float32),
            jax.ShapeDtypeStruct((EXPERT_TOPK, T), jnp.int32),
            jax.ShapeDtypeStruct((EXPERT_TOPK, T), jnp.int32),
            jax.ShapeDtypeStruct((EXPERT_TOPK, T), jnp.float32),
            jax.ShapeDtypeStruct((N_EXPERTS, 128), jnp.int32),
        ],
        scratch_shapes=[pltpu.VMEM((N_EXPERTS, 128), jnp.float32)],
        compiler_params=_cparams(("arbitrary",)),
        name="outproj_router",
    )(o_fox, o_dsa, x2, wo_f, wo_d, g, rw_hi, rw_lo, rb)


def _dispatch_kernel(dest_ref, xn_ref, xg_in_ref, xg_ref, sem, *, tt):
    del xg_in_ref

    def row_copy(t, k):
        return pltpu.make_async_copy(xn_ref.at[pl.ds(t, 1)], xg_ref.at[pl.ds(dest_ref[k, t], 1)], sem)

    def issue(t, c):
        for k in range(EXPERT_TOPK):
            row_copy(t, k).start()
        return c

    def drain(t, c):
        for k in range(EXPERT_TOPK):
            row_copy(t, k).wait()
        return c

    lax.fori_loop(0, tt, issue, 0)
    lax.fori_loop(0, tt, drain, 0)


def _dispatch(dest, xn, n_rows, tt=256):
    T, D = xn.shape
    xg0 = jnp.zeros((n_rows, D), xn.dtype)
    return pl.pallas_call(
        functools.partial(_dispatch_kernel, tt=tt),
        grid=(T // tt,),
        in_specs=[
            pl.BlockSpec((EXPERT_TOPK, tt), lambda i: (0, i), memory_space=pltpu.SMEM),
            pl.BlockSpec((tt, D), lambda i: (i, 0)),
            pl.BlockSpec(memory_space=pl.ANY),
        ],
        out_specs=pl.BlockSpec(memory_space=pl.ANY),
        out_shape=jax.ShapeDtypeStruct((n_rows, D), xn.dtype),
        scratch_shapes=[pltpu.SemaphoreType.DMA(())],
        input_output_aliases={2: 0},
        compiler_params=_cparams(("arbitrary",)),
        name="moe_dispatch",
    )(dest, xn, xg0)


MXU_WIDTH = 256


def _w1_split_kernel(w_ref, g_ref, l_ref):
    half = MXU_WIDTH // 2
    r = lax.broadcasted_iota(jnp.int32, (MXU_WIDTH, MXU_WIDTH), 0)
    c = lax.broadcasted_iota(jnp.int32, (MXU_WIDTH, MXU_WIDTH), 1)
    src = jnp.where(c < half, 2 * c, 2 * (c - half) + 1)
    perm = jnp.where(r == src, 1.0, 0.0).astype(jnp.bfloat16)
    for k in range(w_ref.shape[1] // MXU_WIDTH):
        wk = w_ref[:, k * MXU_WIDTH:(k + 1) * MXU_WIDTH].astype(jnp.bfloat16)
        out = jnp.dot(wk, perm, preferred_element_type=jnp.float32)
        g_ref[:, k * half:(k + 1) * half] = out[:, :half].astype(g_ref.dtype)
        l_ref[:, k * half:(k + 1) * half] = out[:, half:].astype(l_ref.dtype)


def _w1_split(w1, tn=512):
    E, D, F2 = w1.shape
    F = F2 // 2
    return pl.pallas_call(
        _w1_split_kernel,
        grid=(E, F // tn),
        in_specs=[pl.BlockSpec((None, D, 2 * tn), lambda e, j: (e, 0, j))],
        out_specs=[pl.BlockSpec((None, D, tn), lambda e, j: (e, 0, j)),
                   pl.BlockSpec((None, D, tn), lambda e, j: (e, 0, j))],
        out_shape=[jax.ShapeDtypeStruct((E, D, F), jnp.bfloat16),
                   jax.ShapeDtypeStruct((E, D, F), jnp.bfloat16)],
        compiler_params=_cparams(("parallel", "parallel")),
        name="w1_split",
    )(w1)


def _expert_kernel(te_ref, nu_ref, x_ref, w1g_ref, w1l_ref, b1g_ref, b1l_ref, w2_ref, b2_ref,
                   y_ref, xb_ref, acc_ref):
    i = pl.program_id(0)
    f = pl.program_id(1)
    nf = pl.num_programs(1)

    @pl.when(i < nu_ref[0])
    def _():
        @pl.when(f == 0)
        def _():
            xb_ref[...] = x_ref[...].astype(xb_ref.dtype)
            acc_ref[...] = jnp.zeros_like(acc_ref)

        xb = xb_ref[...]
        glu = jnp.dot(xb, w1g_ref[...], preferred_element_type=jnp.float32) + b1g_ref[...]
        lin = jnp.dot(xb, w1l_ref[...], preferred_element_type=jnp.float32) + b1l_ref[...]
        glu = jnp.minimum(glu, SWIGLU_LIMIT)
        lin = jnp.clip(lin, -SWIGLU_LIMIT, SWIGLU_LIMIT)
        act = glu * (1.0 / (1.0 + jnp.exp(-SWIGLU_ALPHA * glu))) * (lin + 1.0)
        acc_ref[...] += jnp.dot(act.astype(jnp.bfloat16), w2_ref[...].astype(jnp.bfloat16),
                                preferred_element_type=jnp.float32)

        @pl.when(f == nf - 1)
        def _():
            y_ref[...] = acc_ref[...] + b2_ref[...]

    @pl.when((i >= nu_ref[0]) & (f == nf - 1))
    def _():
        y_ref[...] = jnp.zeros_like(y_ref)


def _experts(tile_expert, n_used, xg, w1g, w1l, b1g, b1l, w2b, b2, tme, tf=512):
    P, D = xg.shape
    F = w2b.shape[1]
    nf = F // tf
    row = lambda i, f, te, nu: (jnp.minimum(i, nu[0] - 1), 0)
    exp = lambda i, te, nu: te[jnp.minimum(i, nu[0] - 1)]
    fblk = lambda i, f, nu: jnp.where(i < nu[0], f, nf - 1)
    grid_spec = pltpu.PrefetchScalarGridSpec(
        num_scalar_prefetch=2,
        grid=(P // tme, nf),
        in_specs=[
            pl.BlockSpec((tme, D), row),
            pl.BlockSpec((None, D, tf), lambda i, f, te, nu: (exp(i, te, nu), 0, fblk(i, f, nu))),
            pl.BlockSpec((None, D, tf), lambda i, f, te, nu: (exp(i, te, nu), 0, fblk(i, f, nu))),
            pl.BlockSpec((None, 1, tf), lambda i, f, te, nu: (exp(i, te, nu), 0, fblk(i, f, nu))),
            pl.BlockSpec((None, 1, tf), lambda i, f, te, nu: (exp(i, te, nu), 0, fblk(i, f, nu))),
            pl.BlockSpec((None, tf, D), lambda i, f, te, nu: (exp(i, te, nu), fblk(i, f, nu), 0)),
            pl.BlockSpec((None, 1, D), lambda i, f, te, nu: (exp(i, te, nu), 0, 0)),
        ],
        out_specs=pl.BlockSpec((tme, D), lambda i, f, te, nu: (i, 0)),
        scratch_shapes=[pltpu.VMEM((tme, D), jnp.bfloat16), pltpu.VMEM((tme, D), jnp.float32)],
    )
    return pl.pallas_call(
        _expert_kernel,
        grid_spec=grid_spec,
        out_shape=jax.ShapeDtypeStruct((P, D), jnp.float32),
        compiler_params=_cparams(("arbitrary", "arbitrary")),
        name="moe_experts",
    )(tile_expert, n_used, xg, w1g, w1l, b1g, b1l, w2b, b2)


def _combine_kernel(dest_ref, x1_ref, gt_ref, y_ref, o_ref, buf_ref, sem, *, tt):
    def row_copy(t, k):
        return pltpu.make_async_copy(y_ref.at[pl.ds(dest_ref[k, t], 1)],
                                     buf_ref.at[k, pl.ds(t, 1)], sem)

    def issue(t, c):
        for k in range(EXPERT_TOPK):
            row_copy(t, k).start()
        return c

    def drain(t, c):
        for k in range(EXPERT_TOPK):
            row_copy(t, k).wait()
        return c

    lax.fori_loop(0, tt, issue, 0)
    lax.fori_loop(0, tt, drain, 0)
    gt = gt_ref[...]
    out = x1_ref[...]
    for k in range(EXPERT_TOPK):
        out = out + gt[:, k:k + 1] * buf_ref[k]
    o_ref[...] = out


def _combine(dest, x1, gates_t, yrows, tt=128):
    T, D = x1.shape
    return pl.pallas_call(
        functools.partial(_combine_kernel, tt=tt),
        grid=(T // tt,),
        in_specs=[
            pl.BlockSpec((EXPERT_TOPK, tt), lambda i: (0, i), memory_space=pltpu.SMEM),
            pl.BlockSpec((tt, D), lambda i: (i, 0)),
            pl.BlockSpec((tt, 8), lambda i: (i, 0)),
            pl.BlockSpec(memory_space=pl.ANY),
        ],
        out_specs=pl.BlockSpec((tt, D), lambda i: (i, 0)),
        out_shape=jax.ShapeDtypeStruct((T, D), x1.dtype),
        scratch_shapes=[pltpu.VMEM((EXPERT_TOPK, tt, D), jnp.float32), pltpu.SemaphoreType.DMA(())],
        compiler_params=_cparams(("arbitrary",)),
        name="moe_combine",
    )(dest, x1, gates_t, yrows)


def kernel_parts(x, attn_norm_g, w_in, fox_gate_b, fox_q_g, fox_k_g, dsa_q_g, dsa_k_g, idx_k_g,
                 rel_bias, w_out, ffn_norm_g, router_w, router_b, w1, b1, w2, b2):
    B, S, D = x.shape
    T = B * S
    l = 0
    f32 = jnp.float32
    wi = w_in[l]
    o = 0
    cols = {}
    for name, width in (("fq", HEAD_WIDTH), ("fk", HEAD_WIDTH), ("fv", HEAD_WIDTH), ("ff", N_HEADS),
                        ("dq", HEAD_WIDTH), ("dk", HEAD_WIDTH), ("dv", HEAD_WIDTH),
                        ("iq", IDX_HEADS * IDX_DIM), ("ik", IDX_DIM), ("iw", IDX_HEADS)):
        cols[name] = wi[:, o:o + width]
        o += width
    w_big = jnp.concatenate([cols[n] for n in ("fq", "fk", "dq", "dk", "fv", "dv", "iq")],
                            axis=1).astype(jnp.bfloat16)
    pad = jnp.zeros((D, SMALL_W - 2 * IDX_DIM - N_HEADS - IDX_HEADS), f32)
    w_small = jnp.concatenate([cols["ik"], cols["ik"], cols["ff"], cols["iw"], pad],
                              axis=1).astype(jnp.bfloat16)
    head_gains = jnp.stack([fox_q_g[l], fox_k_g[l], dsa_q_g[l], dsa_k_g[l]]).reshape(4, 1, HEAD_DIM)
    gate_b128 = jnp.zeros((1, 128), f32).at[0, :N_HEADS].set(fox_gate_b[l])
    ikg128 = jnp.concatenate([idx_k_g[l], idx_k_g[l]]).reshape(1, 128)

    x2 = x.reshape(T, D)
    big, small = _inproj(x2, attn_norm_g[l].reshape(1, D), w_big, w_small, head_gains)
    big3 = big.reshape(B, S, N_BIG_TILES * HEAD_WIDTH)
    small3 = small.reshape(B, S, SMALL_W)
    ccol, crow, ike, iko = _prep(small3, gate_b128, ikg128)
    tq = 512
    ck5 = crow.reshape(B, N_HEADS, S // tq, 1, tq)
    o_fox = _fox_attention(big3, ccol, ck5, tq=tq)
    topk = min(IDX_TOPK_MAX, S // 4)
    mask = _indexer_mask(big3, small3, ike, iko, topk)
    bias_near = _bias_tiles(rel_bias, tq)
    o_dsa = _dsa_attention(big3, mask, bias_near, rel_bias, tq=tq)
    bf16 = jnp.bfloat16
    wo = w_out[l].astype(bf16)
    rw = router_w[l].T
    rw_hi = rw.astype(bf16)
    rw_lo = (rw - rw_hi.astype(f32)).astype(bf16)
    x1, xn, eidx, rank, gates, cnt = _outproj_router(
        o_fox.reshape(T, HEAD_WIDTH), o_dsa.reshape(T, HEAD_WIDTH), x2, wo[:HEAD_WIDTH], wo[HEAD_WIDTH:],
        ffn_norm_g[l].reshape(1, D), rw_hi, rw_lo, router_b[l].reshape(N_EXPERTS, 1))
    tme = 512
    n_rows = T * EXPERT_TOPK + N_EXPERTS * tme
    n_tiles = n_rows // tme
    counts = cnt[:, 0]
    tiles_e = (counts + tme - 1) // tme
    tile_end = jnp.cumsum(tiles_e)
    pad_start = (tile_end - tiles_e) * tme
    e_ids = jnp.arange(N_EXPERTS, dtype=jnp.int32)
    dest = rank + jnp.sum(jnp.where(eidx[None] == e_ids[:, None, None], pad_start[:, None, None], 0),
                          axis=0)
    tile_expert = jnp.minimum(
        jnp.sum(tile_end[None, :] <= jnp.arange(n_tiles, dtype=jnp.int32)[:, None], axis=1),
        N_EXPERTS - 1).astype(jnp.int32)
    n_used = tile_end[-1:].astype(jnp.int32)
    xg = _dispatch(dest, xn, n_rows)
    F = w2.shape[2]
    w1g, w1l = _w1_split(w1[l])
    b1g = b1[l][:, 0::2].reshape(N_EXPERTS, 1, F)
    b1l = b1[l][:, 1::2].reshape(N_EXPERTS, 1, F)
    yrows = _experts(tile_expert, n_used, xg, w1g, w1l, b1g, b1l, w2[l],
                     b2[l].reshape(N_EXPERTS, 1, D), tme)
    gates_t = jnp.pad(gates.T, ((0, 0), (0, 8 - EXPERT_TOPK)))
    out = _combine(dest, x1, gates_t, yrows)
    return dict(o_fox=o_fox, o_dsa=o_dsa, mask=mask, x1=x1, out=out.reshape(B, S, D))


def kernel(x, attn_norm_g, w_in, fox_gate_b, fox_q_g, fox_k_g, dsa_q_g, dsa_k_g, idx_k_g,
           rel_bias, w_out, ffn_norm_g, router_w, router_b, w1, b1, w2, b2):
    return kernel_parts(x, attn_norm_g, w_in, fox_gate_b, fox_q_g, fox_k_g, dsa_q_g, dsa_k_g, idx_k_g,
                        rel_bias, w_out, ffn_norm_g, router_w, router_b, w1, b1, w2, b2)["out"]
```

```python
import functools
import math

import jax
import jax.numpy as jnp
import numpy as np
from jax import lax
from jax.experimental import pallas as pl
from jax.experimental.pallas import tpu as pltpu

HEAD_DIM = 128
N_HEADS = 8
HEAD_WIDTH = N_HEADS * HEAD_DIM
IDX_HEADS = 16
IDX_DIM = 64
NORM_EPS = 1e-6
FOX_SCALE = HEAD_DIM ** -0.5
NEG_BIG = -1e30
LOG2E = math.log2(math.e)

COL_FQ, COL_FK, COL_DQ, COL_DK, COL_FV, COL_DV, COL_IQ = range(7)
N_BIG_TILES = 7
N_NORM_TILES = 4
SMALL_W = 256

VMEM_LIMIT = 56 * 1024 * 1024


def _cparams(sem):
    return pltpu.CompilerParams(dimension_semantics=sem, vmem_limit_bytes=VMEM_LIMIT)


def _inproj_kernel(x_ref, g_ref, wb_ref, ws_ref, hg_ref, big_ref, small_ref, xn_ref):
    j = pl.program_id(1)

    @pl.when(j == 0)
    def _():
        x = x_ref[...]
        ms = jnp.mean(x * x, axis=-1, keepdims=True)
        xn = (x * lax.rsqrt(ms + NORM_EPS) * g_ref[...]).astype(jnp.bfloat16)
        xn_ref[...] = xn
        small_ref[...] = jnp.dot(xn, ws_ref[...], preferred_element_type=jnp.float32)

    acc = jnp.dot(xn_ref[...], wb_ref[...], preferred_element_type=jnp.float32)

    @pl.when(j < N_NORM_TILES)
    def _():
        hg = hg_ref[...]
        for h in range(N_HEADS):
            t = acc[:, h * HEAD_DIM:(h + 1) * HEAD_DIM]
            ms = jnp.mean(t * t, axis=-1, keepdims=True)
            big_ref[:, h * HEAD_DIM:(h + 1) * HEAD_DIM] = (
                t * lax.rsqrt(ms + NORM_EPS) * hg).astype(big_ref.dtype)

    @pl.when(j >= N_NORM_TILES)
    def _():
        big_ref[...] = acc.astype(big_ref.dtype)


def _inproj(x2, g, w_big, w_small, head_gains, tm=1024):
    T, D = x2.shape
    return pl.pallas_call(
        _inproj_kernel,
        grid=(T // tm, N_BIG_TILES),
        in_specs=[
            pl.BlockSpec((tm, D), lambda i, j: (i, 0)),
            pl.BlockSpec((1, D), lambda i, j: (0, 0)),
            pl.BlockSpec((D, HEAD_WIDTH), lambda i, j: (0, j)),
            pl.BlockSpec((D, SMALL_W), lambda i, j: (0, 0)),
            pl.BlockSpec((None, 1, HEAD_DIM), lambda i, j: (jnp.minimum(j, N_NORM_TILES - 1), 0, 0)),
        ],
        out_specs=[
            pl.BlockSpec((tm, HEAD_WIDTH), lambda i, j: (i, j)),
            pl.BlockSpec((tm, SMALL_W), lambda i, j: (i, 0)),
        ],
        out_shape=[
            jax.ShapeDtypeStruct((T, N_BIG_TILES * HEAD_WIDTH), jnp.bfloat16),
            jax.ShapeDtypeStruct((T, SMALL_W), jnp.float32),
        ],
        scratch_shapes=[pltpu.VMEM((tm, D), jnp.bfloat16)],
        compiler_params=_cparams(("parallel", "arbitrary")),
        name="inproj",
    )(x2, g, w_big, w_small, head_gains)


def _prep_kernel(small_ref, gb_ref, ikg_ref, ccol_ref, crow_ref, ike_ref, iko_ref, *, tk):
    S = small_ref.shape[0]
    z = small_ref[:, 128:256] + gb_ref[...]
    ls = jnp.minimum(z, 0.0) - jnp.log(1.0 + jnp.exp(-jnp.abs(z)))
    row = lax.broadcasted_iota(jnp.int32, (S, 128), 0)
    c = ls
    sh = 1
    while sh < S:
        c = c + jnp.where(row >= sh, pltpu.roll(c, sh, axis=0), 0.0)
        sh *= 2
    ccol_ref[...] = c
    crow_ref[...] = c.T[0:N_HEADS, :]
    ik = small_ref[:, 0:128]
    ms = jnp.sum(ik * ik, axis=-1, keepdims=True) * (1.0 / 128.0)
    ikn = ik * lax.rsqrt(ms + NORM_EPS) * ikg_ref[...]
    lane = lax.broadcasted_iota(jnp.int32, (S, 128), 1)
    ike = jnp.where(lane < IDX_DIM, ikn, 0.0)
    iko = jnp.where(lane >= IDX_DIM, ikn, 0.0)
    for cidx in range(S // tk):
        ike_ref[cidx] = ike[cidx * tk:(cidx + 1) * tk, :].T.astype(ike_ref.dtype)
        iko_ref[cidx] = iko[cidx * tk:(cidx + 1) * tk, :].T.astype(iko_ref.dtype)


def _prep(small3, gate_b128, ikg128, tk=512):
    B, S, _ = small3.shape
    nk = S // tk
    return pl.pallas_call(
        functools.partial(_prep_kernel, tk=tk),
        grid=(B,),
        in_specs=[
            pl.BlockSpec((None, S, SMALL_W), lambda b: (b, 0, 0)),
            pl.BlockSpec((1, 128), lambda b: (0, 0)),
            pl.BlockSpec((1, 128), lambda b: (0, 0)),
        ],
        out_specs=[
            pl.BlockSpec((None, S, 128), lambda b: (b, 0, 0)),
            pl.BlockSpec((None, N_HEADS, S), lambda b: (b, 0, 0)),
            pl.BlockSpec((None, nk, 128, tk), lambda b: (b, 0, 0, 0)),
            pl.BlockSpec((None, nk, 128, tk), lambda b: (b, 0, 0, 0)),
        ],
        out_shape=[
            jax.ShapeDtypeStruct((B, S, 128), jnp.float32),
            jax.ShapeDtypeStruct((B, N_HEADS, S), jnp.float32),
            jax.ShapeDtypeStruct((B, nk, 128, tk), jnp.bfloat16),
            jax.ShapeDtypeStruct((B, nk, 128, tk), jnp.bfloat16),
        ],
        compiler_params=_cparams(("parallel",)),
        name="prep",
    )(small3, gate_b128, ikg128)


def _fox_kernel(q_ref, k_ref, v_ref, ccol_ref, ck_ref, o_ref, acc_ref, *, tq):
    h = pl.program_id(1)
    i = pl.program_id(2)
    lane = lax.broadcasted_iota(jnp.int32, (tq, 128), 1)
    cq = jnp.sum(jnp.where(lane == h, ccol_ref[...], 0.0), axis=-1, keepdims=True) * LOG2E
    q = q_ref[...]

    def qk(j):
        kj = k_ref[pl.ds(pl.multiple_of(j * tq, tq), tq), :]
        return lax.dot_general(q, kj, (((1,), (1,)), ((), ())), preferred_element_type=jnp.float32)

    def update(j, s, m, l, masked):
        vj = v_ref[pl.ds(pl.multiple_of(j * tq, tq), tq), :]
        t = s * (FOX_SCALE * LOG2E) - ck_ref[j] * LOG2E
        if masked:
            r = lax.broadcasted_iota(jnp.int32, (tq, tq), 0)
            c = lax.broadcasted_iota(jnp.int32, (tq, tq), 1)
            t = jnp.where(c <= r, t, NEG_BIG)
        m_new = jnp.maximum(m, jnp.max(t, axis=-1, keepdims=True) + cq)
        p = jnp.exp2(t - (m_new - cq))
        alpha = jnp.exp2(m - m_new)
        l_new = alpha * l + jnp.sum(p, axis=-1, keepdims=True)
        acc_ref[...] = alpha * acc_ref[...] + jnp.dot(
            p.astype(jnp.bfloat16), vj, preferred_element_type=jnp.float32)
        return m_new, l_new

    acc_ref[...] = jnp.zeros_like(acc_ref)
    m0 = jnp.full((tq, 1), NEG_BIG, jnp.float32)
    l0 = jnp.zeros((tq, 1), jnp.float32)
    m, l = lax.fori_loop(0, i, lambda j, c: update(j, qk(j), c[0], c[1], False), (m0, l0))
    m, l = update(i, qk(i), m, l, True)
    o_ref[...] = (acc_ref[...] / l).astype(o_ref.dtype)


def _fox_attention(big3, ccol, ck5, tq=512):
    B, S, _ = big3.shape
    nq = S // tq
    hb = HEAD_WIDTH // HEAD_DIM
    return pl.pallas_call(
        functools.partial(_fox_kernel, tq=tq),
        grid=(B, N_HEADS, nq),
        in_specs=[
            pl.BlockSpec((None, tq, HEAD_DIM), lambda b, h, i: (b, i, COL_FQ * hb + h)),
            pl.BlockSpec((None, S, HEAD_DIM), lambda b, h, i: (b, 0, COL_FK * hb + h)),
            pl.BlockSpec((None, S, HEAD_DIM), lambda b, h, i: (b, 0, COL_FV * hb + h)),
            pl.BlockSpec((None, tq, 128), lambda b, h, i: (b, i, 0)),
            pl.BlockSpec((None, None, nq, 1, tq), lambda b, h, i: (b, h, 0, 0, 0)),
        ],
        out_specs=pl.BlockSpec((None, tq, HEAD_DIM), lambda b, h, i: (b, i, h)),
        out_shape=jax.ShapeDtypeStruct((B, S, HEAD_WIDTH), jnp.bfloat16),
        scratch_shapes=[pltpu.VMEM((tq, HEAD_DIM), jnp.float32)],
        compiler_params=_cparams(("parallel", "parallel", "arbitrary")),
        name="fox_attn",
    )(big3, big3, big3, ccol, ck5)


CHUNK = 64
IDX_TOPK_MAX = 256
IDX_SCALE = (IDX_DIM ** -0.5) * (IDX_HEADS ** -0.5)
INT_MIN = -2 ** 31


def _indexer_kernel(iq_ref, w_ref, ike_ref, iko_ref, mask_ref, key_ref, hi_ref, lo_ref, *, tq, tk, topk):
    i = pl.program_id(1)
    nc = key_ref.shape[0]
    q0 = i * tq
    nch = (q0 + tq + tk - 1) // tk
    rowpos = q0 + lax.broadcasted_iota(jnp.int32, (tq, 1), 0)
    limit = (rowpos // CHUNK + 1) * CHUNK
    wv = w_ref[...]
    wcols = [wv[:, N_HEADS + h:N_HEADS + h + 1] for h in range(IDX_HEADS)]
    lane_pos = lax.broadcasted_iota(jnp.int32, (tq, tk), 1)

    def score_chunk(c, carry):
        acc = jnp.zeros((tq, tk), jnp.float32)
        for p in range(IDX_HEADS // 2):
            qp = iq_ref[:, p * 128:(p + 1) * 128]
            de = jnp.dot(qp, ike_ref[c], preferred_element_type=jnp.float32)
            do = jnp.dot(qp, iko_ref[c], preferred_element_type=jnp.float32)
            acc = acc + wcols[2 * p] * jnp.maximum(de, 0.0) + wcols[2 * p + 1] * jnp.maximum(do, 0.0)
        sc = jnp.where(lane_pos + c * tk < limit, acc * IDX_SCALE, -jnp.inf)
        bits = pltpu.bitcast(sc, jnp.int32)
        key = bits ^ ((bits >> 31) & 0x7FFFFFFF)
        key_ref[c] = key
        hi_ref[c] = (key >> 16).astype(jnp.int16)
        return carry

    lax.fori_loop(0, nch, score_chunk, 0)

    ones = jnp.ones((128, 128), jnp.bfloat16)
    lanes = [slice(s * 128, (s + 1) * 128) for s in range(tk // 128)]

    def count_rows(ref, pred):
        def body(c, cnt):
            k = ref[c]
            for ls in lanes:
                cnt = cnt + jnp.where(pred(k[:, ls]), jnp.int16(1), jnp.int16(0))
            return cnt

        cnt = lax.fori_loop(0, nch, body, jnp.zeros((tq, 128), jnp.int16))
        return jnp.dot(cnt.astype(jnp.float32).astype(jnp.bfloat16), ones,
                       preferred_element_type=jnp.float32)

    def search16(ref, need):
        def step(it, thr):
            cand = thr + (jnp.int32(1) << (15 - it))
            cand16 = cand.astype(jnp.int16)
            total = count_rows(ref, lambda k: k >= cand16)
            return jnp.where(total >= need, cand, thr)

        return lax.fori_loop(0, 16, step, jnp.full((tq, 128), -32768, jnp.int32))

    th = search16(hi_ref, float(topk))
    th16 = th.astype(jnp.int16)
    need_lo = float(topk) - count_rows(hi_ref, lambda k: k > th16)

    def low_chunk(c, carry):
        key = key_ref[c]
        for ls in lanes:
            k = key[:, ls]
            lo = (k & 0xFFFF) - 32768
            lo_ref[c, :, ls] = jnp.where((k >> 16) == th, lo, -32768).astype(jnp.int16)
        return carry

    lax.fori_loop(0, nch, low_chunk, 0)
    tl = search16(lo_ref, need_lo)
    thr = (th << 16) + (tl + 32768)

    def write_chunk(c, carry):
        key = key_ref[c]
        admissible = jnp.where(lane_pos + c * tk < limit, 0.0, NEG_BIG)
        for ls in lanes:
            mask_ref[c, :, ls] = jnp.where(key[:, ls] >= thr, admissible[:, ls],
                                           NEG_BIG).astype(mask_ref.dtype)
        return carry

    lax.fori_loop(0, nch, write_chunk, 0)

    def fill_chunk(c, carry):
        mask_ref[c] = jnp.full((tq, tk), NEG_BIG, mask_ref.dtype)
        return carry

    lax.fori_loop(nch, nc, fill_chunk, 0)


def _indexer_mask(big3, small3, ike, iko, topk, tq=512):
    B, S, _ = big3.shape
    nc, _, tk = ike.shape[1:]
    return pl.pallas_call(
        functools.partial(_indexer_kernel, tq=tq, tk=tk, topk=topk),
        grid=(B, S // tq),
        in_specs=[
            pl.BlockSpec((None, tq, HEAD_WIDTH), lambda b, i: (b, i, COL_IQ)),
            pl.BlockSpec((None, tq, 128), lambda b, i: (b, i, 1)),
            pl.BlockSpec((None, nc, 128, tk), lambda b, i: (b, 0, 0, 0)),
            pl.BlockSpec((None, nc, 128, tk), lambda b, i: (b, 0, 0, 0)),
        ],
        out_specs=pl.BlockSpec((None, nc, tq, tk), lambda b, i: (b, 0, i, 0)),
        out_shape=jax.ShapeDtypeStruct((B, nc, S, tk), jnp.bfloat16),
        scratch_shapes=[pltpu.VMEM((nc, tq, tk), jnp.int32), pltpu.VMEM((nc, tq, tk), jnp.int16),
                        pltpu.VMEM((nc, tq, tk), jnp.int16)],
        compiler_params=_cparams(("parallel", "arbitrary")),
        name="indexer_mask",
    )(big3, small3, ike, iko)


REL_BUCKETS = 32
REL_MAX_DIST = 128
FAR_BUCKET = REL_BUCKETS // 2 - 1


def _t5_bucket(rel):
    half = REL_BUCKETS // 2
    max_exact = half // 2
    ret = jnp.where(rel > 0, half, 0)
    n = jnp.abs(rel)
    nf = jnp.maximum(n, 1).astype(jnp.float32)
    large = max_exact + (jnp.log(nf / max_exact) / math.log(REL_MAX_DIST / max_exact)
                         * (half - max_exact)).astype(jnp.int32)
    large = jnp.minimum(large, half - 1)
    return ret + jnp.where(n < max_exact, n, large)


def _bias_tile_kernel(tab_ref, bucket_ref, out_ref):
    h = pl.program_id(0)
    for d in range(2):
        bk = bucket_ref[d]
        acc = jnp.zeros(bk.shape, jnp.float32)
        for b in range(REL_BUCKETS):
            acc = jnp.where(bk == b, tab_ref[h, b], acc)
        out_ref[d] = acc * LOG2E


def _bias_tiles(rel_bias, tq):
    r = jnp.arange(tq, dtype=jnp.int32)
    rel = r[None, :] - r[:, None]
    buckets = jnp.stack([_t5_bucket(rel), _t5_bucket(rel - tq)])
    return pl.pallas_call(
        _bias_tile_kernel,
        grid=(N_HEADS,),
        in_specs=[
            pl.BlockSpec(memory_space=pltpu.SMEM),
            pl.BlockSpec((2, tq, tq), lambda h: (0, 0, 0)),
        ],
        out_specs=pl.BlockSpec((None, 2, tq, tq), lambda h: (h, 0, 0, 0)),
        out_shape=jax.ShapeDtypeStruct((N_HEADS, 2, tq, tq), jnp.float32),
        compiler_params=_cparams(("parallel",)),
        name="t5_bias_tiles",
    )(rel_bias.T.astype(jnp.float32), buckets)


def _dsa_kernel(tab_ref, q_ref, k_ref, v_ref, mask_ref, bias_ref, o_ref, acc_ref, *, tq):
    i = pl.program_id(1)
    h = pl.program_id(2)
    far_bias = tab_ref[h, FAR_BUCKET] * LOG2E
    q = q_ref[...]

    def qk(j):
        kj = k_ref[pl.ds(pl.multiple_of(j * tq, tq), tq), :]
        return lax.dot_general(q, kj, (((1,), (1,)), ((), ())), preferred_element_type=jnp.float32)

    def update(j, s, m, l, near):
        vj = v_ref[pl.ds(pl.multiple_of(j * tq, tq), tq), :]
        t = s * (FOX_SCALE * LOG2E) + mask_ref[j].astype(jnp.float32)
        if near is None:
            shift = far_bias
        else:
            t = t + bias_ref[near]
            shift = 0.0
        m_new = jnp.maximum(m, jnp.max(t, axis=-1, keepdims=True) + shift)
        p = jnp.exp2(t - (m_new - shift))
        alpha = jnp.exp2(m - m_new)
        l_new = alpha * l + jnp.sum(p, axis=-1, keepdims=True)
        acc_ref[...] = alpha * acc_ref[...] + jnp.dot(
            p.astype(jnp.bfloat16), vj, preferred_element_type=jnp.float32)
        return m_new, l_new

    acc_ref[...] = jnp.zeros_like(acc_ref)
    m0 = jnp.full((tq, 1), NEG_BIG, jnp.float32)
    l0 = jnp.zeros((tq, 1), jnp.float32)
    carry = lax.fori_loop(0, i - 1, lambda j, c: update(j, qk(j), c[0], c[1], None), (m0, l0))
    m, l = lax.cond(i >= 1, lambda c: update(i - 1, qk(i - 1), c[0], c[1], 1), lambda c: c, carry)
    m, l = update(i, qk(i), m, l, 0)
    o_ref[...] = (acc_ref[...] / l).astype(o_ref.dtype)


def _dsa_attention(big3, mask, bias_near, rel_bias, tq=512):
    B, S, _ = big3.shape
    nq = S // tq
    nc = mask.shape[1]
    assert mask.shape[3] == tq and tq >= REL_MAX_DIST and tq % CHUNK == 0
    hb = HEAD_WIDTH // HEAD_DIM
    return pl.pallas_call(
        functools.partial(_dsa_kernel, tq=tq),
        grid=(B, nq, N_HEADS),
        in_specs=[
            pl.BlockSpec(memory_space=pltpu.SMEM),
            pl.BlockSpec((None, tq, HEAD_DIM), lambda b, i, h: (b, i, COL_DQ * hb + h)),
            pl.BlockSpec((None, S, HEAD_DIM), lambda b, i, h: (b, 0, COL_DK * hb + h)),
            pl.BlockSpec((None, S, HEAD_DIM), lambda b, i, h: (b, 0, COL_DV * hb + h)),
            pl.BlockSpec((None, nc, tq, tq), lambda b, i, h: (b, 0, i, 0)),
            pl.BlockSpec((None, 2, tq, tq), lambda b, i, h: (h, 0, 0, 0)),
        ],
        out_specs=pl.BlockSpec((None, tq, HEAD_DIM), lambda b, i, h: (b, i, h)),
        out_shape=jax.ShapeDtypeStruct((B, S, HEAD_WIDTH), jnp.bfloat16),
        scratch_shapes=[pltpu.VMEM((tq, HEAD_DIM), jnp.float32)],
        compiler_params=_cparams(("parallel", "parallel", "arbitrary")),
        name="dsa_attn",
    )(rel_bias.T.astype(jnp.float32), big3, big3, big3, mask, bias_near)


N_EXPERTS = 32
EXPERT_TOPK = 4
SWIGLU_ALPHA = 1.702
SWIGLU_LIMIT = 7.0


def _outproj_router_kernel(of_ref, od_ref, x_ref, wof_ref, wod_ref, g_ref, rwh_ref, rwl_ref, rb_ref,
                           x1_ref, xn_ref, eidx_ref, rank_ref, gate_ref, cnt_ref, carry_ref, *, tm):
    i = pl.program_id(0)

    @pl.when(i == 0)
    def _():
        carry_ref[...] = jnp.zeros_like(carry_ref)

    x1 = (x_ref[...]
          + jnp.dot(of_ref[...], wof_ref[...], preferred_element_type=jnp.float32)
          + jnp.dot(od_ref[...], wod_ref[...], preferred_element_type=jnp.float32))
    x1_ref[...] = x1
    ms = jnp.mean(x1 * x1, axis=-1, keepdims=True)
    xn = x1 * lax.rsqrt(ms + NORM_EPS) * g_ref[...]
    xn_ref[...] = xn
    xh = xn.astype(jnp.bfloat16)
    xl = (xn - xh.astype(jnp.float32)).astype(jnp.bfloat16)
    nt = (((1,), (1,)), ((), ()))
    logits = (lax.dot_general(rwh_ref[...], xh, nt, preferred_element_type=jnp.float32)
              + lax.dot_general(rwh_ref[...], xl, nt, preferred_element_type=jnp.float32)
              + lax.dot_general(rwl_ref[...], xh, nt, preferred_element_type=jnp.float32)
              + rb_ref[...])
    eio = lax.broadcasted_iota(jnp.int32, (N_EXPERTS, tm), 0)
    work = logits
    vals, idxs = [], []
    multihot = jnp.zeros((N_EXPERTS, tm), jnp.float32)
    for _ in range(EXPERT_TOPK):
        mx = jnp.max(work, axis=0, keepdims=True)
        ix = jnp.min(jnp.where(work == mx, eio, N_EXPERTS), axis=0, keepdims=True)
        hit = eio == ix
        multihot = jnp.where(hit, 1.0, multihot)
        work = jnp.where(hit, -jnp.inf, work)
        vals.append(mx)
        idxs.append(ix)
    ex = [jnp.exp(v - vals[0]) for v in vals]
    den = ex[0] + ex[1] + ex[2] + ex[3]
    a = lax.broadcasted_iota(jnp.int32, (tm, tm), 0)
    b = lax.broadcasted_iota(jnp.int32, (tm, tm), 1)
    upper = jnp.where(a < b, 1.0, 0.0).astype(jnp.bfloat16)
    before = jnp.dot(multihot.astype(jnp.bfloat16), upper,
                     preferred_element_type=jnp.float32) + carry_ref[:, 0:1]
    for r in range(EXPERT_TOPK):
        eidx_ref[r:r + 1, :] = idxs[r]
        gate_ref[r:r + 1, :] = ex[r] / den
        rank_ref[r:r + 1, :] = jnp.sum(jnp.where(eio == idxs[r], before, 0.0),
                                       axis=0, keepdims=True).astype(jnp.int32)
    carry_ref[...] = carry_ref[...] + jnp.sum(multihot, axis=1, keepdims=True)
    cnt_ref[...] = carry_ref[...].astype(jnp.int32)


def _outproj_router(o_fox, o_dsa, x2, wo_f, wo_d, g, rw_hi, rw_lo, rb, tm=512):
    T, D = x2.shape
    hw = o_fox.shape[1]
    full = lambda shape: pl.BlockSpec(shape, lambda i: tuple(0 for _ in shape))
    return pl.pallas_call(
        functools.partial(_outproj_router_kernel, tm=tm),
        grid=(T // tm,),
        in_specs=[
            pl.BlockSpec((tm, hw), lambda i: (i, 0)),
            pl.BlockSpec((tm, hw), lambda i: (i, 0)),
            pl.BlockSpec((tm, D), lambda i: (i, 0)),
            full((hw, D)), full((hw, D)), full((1, D)),
            full((N_EXPERTS, D)), full((N_EXPERTS, D)), full((N_EXPERTS, 1)),
        ],
        out_specs=[
            pl.BlockSpec((tm, D), lambda i: (i, 0)),
            pl.BlockSpec((tm, D), lambda i: (i, 0)),
            pl.BlockSpec((EXPERT_TOPK, tm), lambda i: (0, i)),
            pl.BlockSpec((EXPERT_TOPK, tm), lambda i: (0, i)),
            pl.BlockSpec((EXPERT_TOPK, tm), lambda i: (0, i)),
            full((N_EXPERTS, 128)),
        ],
        out_shape=[
            jax.ShapeDtypeStruct((T, D), jnp.float32),
            jax.ShapeDtypeStruct((T, D), jnp.float32),
            jax.ShapeDtypeStruct((EXPERT_TOPK, T), jnp.int32),
            jax.ShapeDtypeStruct((EXPERT_TOPK, T), jnp.int32),
            jax.ShapeDtypeStruct((EXPERT_TOPK, T), jnp.float32),
            jax.ShapeDtypeStruct((N_EXPERTS, 128), jnp.int32),
        ],
        scratch_shapes=[pltpu.VMEM((N_EXPERTS, 128), jnp.float32)],
        compiler_params=_cparams(("arbitrary",)),
        name="outproj_router",
    )(o_fox, o_dsa, x2, wo_f, wo_d, g, rw_hi, rw_lo, rb)


def _dispatch_kernel(dest_ref, xn_ref, xg_in_ref, xg_ref, sem, *, tt):
    del xg_in_ref

    def row_copy(t, k):
        return pltpu.make_async_copy(xn_ref.at[pl.ds(t, 1)], xg_ref.at[pl.ds(dest_ref[k, t], 1)], sem)

    def issue(t, c):
        for k in range(EXPERT_TOPK):
            row_copy(t, k).start()
        return c

    def drain(t, c):
        for k in range(EXPERT_TOPK):
            row_copy(t, k).wait()
        return c

    lax.fori_loop(0, tt, issue, 0)
    lax.fori_loop(0, tt, drain, 0)


def _dispatch(dest, xn, n_rows, tt=256):
    T, D = xn.shape
    xg0 = jnp.zeros((n_rows, D), xn.dtype)
    return pl.pallas_call(
        functools.partial(_dispatch_kernel, tt=tt),
        grid=(T // tt,),
        in_specs=[
            pl.BlockSpec((EXPERT_TOPK, tt), lambda i: (0, i), memory_space=pltpu.SMEM),
            pl.BlockSpec((tt, D), lambda i: (i, 0)),
            pl.BlockSpec(memory_space=pl.ANY),
        ],
        out_specs=pl.BlockSpec(memory_space=pl.ANY),
        out_shape=jax.ShapeDtypeStruct((n_rows, D), xn.dtype),
        scratch_shapes=[pltpu.SemaphoreType.DMA(())],
        input_output_aliases={2: 0},
        compiler_params=_cparams(("arbitrary",)),
        name="moe_dispatch",
    )(dest, xn, xg0)


MXU_WIDTH = 256


def _w1_split_kernel(w_ref, g_ref, l_ref):
    half = MXU_WIDTH // 2
    r = lax.broadcasted_iota(jnp.int32, (MXU_WIDTH, MXU_WIDTH), 0)
    c = lax.broadcasted_iota(jnp.int32, (MXU_WIDTH, MXU_WIDTH), 1)
    src = jnp.where(c < half, 2 * c, 2 * (c - half) + 1)
    perm = jnp.where(r == src, 1.0, 0.0).astype(jnp.bfloat16)
    for k in range(w_ref.shape[1] // MXU_WIDTH):
        wk = w_ref[:, k * MXU_WIDTH:(k + 1) * MXU_WIDTH].astype(jnp.bfloat16)
        out = jnp.dot(wk, perm, preferred_element_type=jnp.float32)
        g_ref[:, k * half:(k + 1) * half] = out[:, :half].astype(g_ref.dtype)
        l_ref[:, k * half:(k + 1) * half] = out[:, half:].astype(l_ref.dtype)


def _w1_split(w1, tn=512):
    E, D, F2 = w1.shape
    F = F2 // 2
    return pl.pallas_call(
        _w1_split_kernel,
        grid=(E, F // tn),
        in_specs=[pl.BlockSpec((None, D, 2 * tn), lambda e, j: (e, 0, j))],
        out_specs=[pl.BlockSpec((None, D, tn), lambda e, j: (e, 0, j)),
                   pl.BlockSpec((None, D, tn), lambda e, j: (e, 0, j))],
        out_shape=[jax.ShapeDtypeStruct((E, D, F), jnp.bfloat16),
                   jax.ShapeDtypeStruct((E, D, F), jnp.bfloat16)],
        compiler_params=_cparams(("parallel", "parallel")),
        name="w1_split",
    )(w1)


def _expert_kernel(te_ref, nu_ref, x_ref, w1g_ref, w1l_ref, b1g_ref, b1l_ref, w2_ref, b2_ref,
                   y_ref, xb_ref, acc_ref):
    i = pl.program_id(0)
    f = pl.program_id(1)
    nf = pl.num_programs(1)

    @pl.when(i < nu_ref[0])
    def _():
        @pl.when(f == 0)
        def _():
            xb_ref[...] = x_ref[...].astype(xb_ref.dtype)
            acc_ref[...] = jnp.zeros_like(acc_ref)

        xb = xb_ref[...]
        glu = jnp.dot(xb, w1g_ref[...], preferred_element_type=jnp.float32) + b1g_ref[...]
        lin = jnp.dot(xb, w1l_ref[...], preferred_element_type=jnp.float32) + b1l_ref[...]
        glu = jnp.minimum(glu, SWIGLU_LIMIT)
        lin = jnp.clip(lin, -SWIGLU_LIMIT, SWIGLU_LIMIT)
        act = glu * (1.0 / (1.0 + jnp.exp(-SWIGLU_ALPHA * glu))) * (lin + 1.0)
        acc_ref[...] += jnp.dot(act.astype(jnp.bfloat16), w2_ref[...].astype(jnp.bfloat16),
                                preferred_element_type=jnp.float32)

        @pl.when(f == nf - 1)
        def _():
            y_ref[...] = acc_ref[...] + b2_ref[...]

    @pl.when((i >= nu_ref[0]) & (f == nf - 1))
    def _():
        y_ref[...] = jnp.zeros_like(y_ref)


def _experts(tile_expert, n_used, xg, w1g, w1l, b1g, b1l, w2b, b2, tme, tf=512):
    P, D = xg.shape
    F = w2b.shape[1]
    nf = F // tf
    row = lambda i, f, te, nu: (jnp.minimum(i, nu[0] - 1), 0)
    exp = lambda i, te, nu: te[jnp.minimum(i, nu[0] - 1)]
    fblk = lambda i, f, nu: jnp.where(i < nu[0], f, nf - 1)
    grid_spec = pltpu.PrefetchScalarGridSpec(
        num_scalar_prefetch=2,
        grid=(P // tme, nf),
        in_specs=[
            pl.BlockSpec((tme, D), row),
            pl.BlockSpec((None, D, tf), lambda i, f, te, nu: (exp(i, te, nu), 0, fblk(i, f, nu))),
            pl.BlockSpec((None, D, tf), lambda i, f, te, nu: (exp(i, te, nu), 0, fblk(i, f, nu))),
            pl.BlockSpec((None, 1, tf), lambda i, f, te, nu: (exp(i, te, nu), 0, fblk(i, f, nu))),
            pl.BlockSpec((None, 1, tf), lambda i, f, te, nu: (exp(i, te, nu), 0, fblk(i, f, nu))),
            pl.BlockSpec((None, tf, D), lambda i, f, te, nu: (exp(i, te, nu), fblk(i, f, nu), 0)),
            pl.BlockSpec((None, 1, D), lambda i, f, te, nu: (exp(i, te, nu), 0, 0)),
        ],
        out_specs=pl.BlockSpec((tme, D), lambda i, f, te, nu: (i, 0)),
        scratch_shapes=[pltpu.VMEM((tme, D), jnp.bfloat16), pltpu.VMEM((tme, D), jnp.float32)],
    )
    return pl.pallas_call(
        _expert_kernel,
        grid_spec=grid_spec,
        out_shape=jax.ShapeDtypeStruct((P, D), jnp.float32),
        compiler_params=_cparams(("arbitrary", "arbitrary")),
        name="moe_experts",
    )(tile_expert, n_used, xg, w1g, w1l, b1g, b1l, w2b, b2)


def _combine_kernel(dest_ref, x1_ref, gt_ref, y_ref, o_ref, buf_ref, sem, *, tt):
    def row_copy(t, k):
        return pltpu.make_async_copy(y_ref.at[pl.ds(dest_ref[k, t], 1)],
                                     buf_ref.at[k, pl.ds(t, 1)], sem)

    def issue(t, c):
        for k in range(EXPERT_TOPK):
            row_copy(t, k).start()
        return c

    def drain(t, c):
        for k in range(EXPERT_TOPK):
            row_copy(t, k).wait()
        return c

    lax.fori_loop(0, tt, issue, 0)
    lax.fori_loop(0, tt, drain, 0)
    gt = gt_ref[...]
    out = x1_ref[...]
    for k in range(EXPERT_TOPK):
        out = out + gt[:, k:k + 1] * buf_ref[k]
    o_ref[...] = out


def _combine(dest, x1, gates_t, yrows, tt=128):
    T, D = x1.shape
    return pl.pallas_call(
        functools.partial(_combine_kernel, tt=tt),
        grid=(T // tt,),
        in_specs=[
            pl.BlockSpec((EXPERT_TOPK, tt), lambda i: (0, i), memory_space=pltpu.SMEM),
            pl.BlockSpec((tt, D), lambda i: (i, 0)),
            pl.BlockSpec((tt, 8), lambda i: (i, 0)),
            pl.BlockSpec(memory_space=pl.ANY),
        ],
        out_specs=pl.BlockSpec((tt, D), lambda i: (i, 0)),
        out_shape=jax.ShapeDtypeStruct((T, D), x1.dtype),
        scratch_shapes=[pltpu.VMEM((EXPERT_TOPK, tt, D), jnp.float32), pltpu.SemaphoreType.DMA(())],
        compiler_params=_cparams(("arbitrary",)),
        name="moe_combine",
    )(dest, x1, gates_t, yrows)


def kernel_parts(x, attn_norm_g, w_in, fox_gate_b, fox_q_g, fox_k_g, dsa_q_g, dsa_k_g, idx_k_g,
                 rel_bias, w_out, ffn_norm_g, router_w, router_b, w1, b1, w2, b2):
    B, S, D = x.shape
    T = B * S
    l = 0
    f32 = jnp.float32
    wi = w_in[l]
    o = 0
    cols = {}
    for name, width in (("fq", HEAD_WIDTH), ("fk", HEAD_WIDTH), ("fv", HEAD_WIDTH), ("ff", N_HEADS),
                        ("dq", HEAD_WIDTH), ("dk", HEAD_WIDTH), ("dv", HEAD_WIDTH),
                        ("iq", IDX_HEADS * IDX_DIM), ("ik", IDX_DIM), ("iw", IDX_HEADS)):
        cols[name] = wi[:, o:o + width]
        o += width
    w_big = jnp.concatenate([cols[n] for n in ("fq", "fk", "dq", "dk", "fv", "dv", "iq")],
                            axis=1).astype(jnp.bfloat16)
    pad = jnp.zeros((D, SMALL_W - 2 * IDX_DIM - N_HEADS - IDX_HEADS), f32)
    w_small = jnp.concatenate([cols["ik"], cols["ik"], cols["ff"], cols["iw"], pad],
                              axis=1).astype(jnp.bfloat16)
    head_gains = jnp.stack([fox_q_g[l], fox_k_g[l], dsa_q_g[l], dsa_k_g[l]]).reshape(4, 1, HEAD_DIM)
    gate_b128 = jnp.zeros((1, 128), f32).at[0, :N_HEADS].set(fox_gate_b[l])
    ikg128 = jnp.concatenate([idx_k_g[l], idx_k_g[l]]).reshape(1, 128)

    x2 = x.reshape(T, D)
    big, small = _inproj(x2, attn_norm_g[l].reshape(1, D), w_big, w_small, head_gains)
    big3 = big.reshape(B, S, N_BIG_TILES * HEAD_WIDTH)
    small3 = small.reshape(B, S, SMALL_W)
    ccol, crow, ike, iko = _prep(small3, gate_b128, ikg128)
    tq = 512
    ck5 = crow.reshape(B, N_HEADS, S // tq, 1, tq)
    o_fox = _fox_attention(big3, ccol, ck5, tq=tq)
    topk = min(IDX_TOPK_MAX, S // 4)
    mask = _indexer_mask(big3, small3, ike, iko, topk)
    bias_near = _bias_tiles(rel_bias, tq)
    o_dsa = _dsa_attention(big3, mask, bias_near, rel_bias, tq=tq)
    bf16 = jnp.bfloat16
    wo = w_out[l].astype(bf16)
    rw = router_w[l].T
    rw_hi = rw.astype(bf16)
    rw_lo = (rw - rw_hi.astype(f32)).astype(bf16)
    x1, xn, eidx, rank, gates, cnt = _outproj_router(
        o_fox.reshape(T, HEAD_WIDTH), o_dsa.reshape(T, HEAD_WIDTH), x2, wo[:HEAD_WIDTH], wo[HEAD_WIDTH:],
        ffn_norm_g[l].reshape(1, D), rw_hi, rw_lo, router_b[l].reshape(N_EXPERTS, 1))
    tme = 512
    n_rows = T * EXPERT_TOPK + N_EXPERTS * tme
    n_tiles = n_rows // tme
    counts = cnt[:, 0]
    tiles_e = (counts + tme - 1) // tme
    tile_end = jnp.cumsum(tiles_e)
    pad_start = (tile_end - tiles_e) * tme
    e_ids = jnp.arange(N_EXPERTS, dtype=jnp.int32)
    dest = rank + jnp.sum(jnp.where(eidx[None] == e_ids[:, None, None], pad_start[:, None, None], 0),
                          axis=0)
    tile_expert = jnp.minimum(
        jnp.sum(tile_end[None, :] <= jnp.arange(n_tiles, dtype=jnp.int32)[:, None], axis=1),
        N_EXPERTS - 1).astype(jnp.int32)
    n_used = tile_end[-1:].astype(jnp.int32)
    xg = _dispatch(dest, xn, n_rows)
    F = w2.shape[2]
    w1g, w1l = _w1_split(w1[l])
    b1g = b1[l][:, 0::2].reshape(N_EXPERTS, 1, F)
    b1l = b1[l][:, 1::2].reshape(N_EXPERTS, 1, F)
    yrows = _experts(tile_expert, n_used, xg, w1g, w1l, b1g, b1l, w2[l],
                     b2[l].reshape(N_EXPERTS, 1, D), tme)
    gates_t = jnp.pad(gates.T, ((0, 0), (0, 8 - EXPERT_TOPK)))
    out = _combine(dest, x1, gates_t, yrows)
    return dict(o_fox=o_fox, o_dsa=o_dsa, mask=mask, x1=x1, out=out.reshape(B, S, D))


def kernel(x, attn_norm_g, w_in, fox_gate_b, fox_q_g, fox_k_g, dsa_q_g, dsa_k_g, idx_k_g,
           rel_bias, w_out, ffn_norm_g, router_w, router_b, w1, b1, w2, b2):
    return kernel_parts(x, attn_norm_g, w_in, fox_gate_b, fox_q_g, fox_k_g, dsa_q_g, dsa_k_g, idx_k_g,
                        rel_bias, w_out, ffn_norm_g, router_w, router_b, w1, b1, w2, b2)["out"]
```

```python
import functools
import math

import jax
import jax.numpy as jnp
import numpy as np
from jax import lax
from jax.experimental import pallas as pl
from jax.experimental.pallas import tpu as pltpu

HEAD_DIM = 128
N_HEADS = 8
HEAD_WIDTH = N_HEADS * HEAD_DIM
IDX_HEADS = 16
IDX_DIM = 64
NORM_EPS = 1e-6
FOX_SCALE = HEAD_DIM ** -0.5
NEG_BIG = -1e30
LOG2E = math.log2(math.e)

COL_FQ, COL_FK, COL_DQ, COL_DK, COL_FV, COL_DV, COL_IQ = range(7)
N_BIG_TILES = 7
N_NORM_TILES = 4
SMALL_W = 256

VMEM_LIMIT = 56 * 1024 * 1024


def _cparams(sem):
    return pltpu.CompilerParams(dimension_semantics=sem, vmem_limit_bytes=VMEM_LIMIT)


def _inproj_kernel(x_ref, g_ref, wb_ref, ws_ref, hg_ref, big_ref, small_ref, xn_ref):
    j = pl.program_id(1)

    @pl.when(j == 0)
    def _():
        x = x_ref[...]
        ms = jnp.mean(x * x, axis=-1, keepdims=True)
        xn = (x * lax.rsqrt(ms + NORM_EPS) * g_ref[...]).astype(jnp.bfloat16)
        xn_ref[...] = xn
        small_ref[...] = jnp.dot(xn, ws_ref[...], preferred_element_type=jnp.float32)

    @pl.when(j < N_NORM_TILES)
    def _():
        hg = hg_ref[...]
        for c0 in range(0, HEAD_WIDTH, 2 * HEAD_DIM):
            acc = jnp.dot(xn_ref[...], wb_ref[:, c0:c0 + 2 * HEAD_DIM],
                          preferred_element_type=jnp.float32)
            for c in (0, HEAD_DIM):
                t = acc[:, c:c + HEAD_DIM]
                ms = jnp.mean(t * t, axis=-1, keepdims=True)
                big_ref[:, c0 + c:c0 + c + HEAD_DIM] = (
                    t * lax.rsqrt(ms + NORM_EPS) * hg).astype(big_ref.dtype)

    @pl.when(j >= N_NORM_TILES)
    def _():
        big_ref[...] = jnp.dot(xn_ref[...], wb_ref[...],
                               preferred_element_type=jnp.float32).astype(big_ref.dtype)


def _inproj(x2, g, w_big, w_small, head_gains, tm=1024):
    T, D = x2.shape
    return pl.pallas_call(
        _inproj_kernel,
        grid=(T // tm, N_BIG_TILES),
        in_specs=[
            pl.BlockSpec((tm, D), lambda i, j: (i, 0)),
            pl.BlockSpec((1, D), lambda i, j: (0, 0)),
            pl.BlockSpec((D, HEAD_WIDTH), lambda i, j: (0, j)),
            pl.BlockSpec((D, SMALL_W), lambda i, j: (0, 0)),
            pl.BlockSpec((None, 1, HEAD_DIM), lambda i, j: (jnp.minimum(j, N_NORM_TILES - 1), 0, 0)),
        ],
        out_specs=[
            pl.BlockSpec((tm, HEAD_WIDTH), lambda i, j: (i, j)),
            pl.BlockSpec((tm, SMALL_W), lambda i, j: (i, 0)),
        ],
        out_shape=[
            jax.ShapeDtypeStruct((T, N_BIG_TILES * HEAD_WIDTH), jnp.bfloat16),
            jax.ShapeDtypeStruct((T, SMALL_W), jnp.float32),
        ],
        scratch_shapes=[pltpu.VMEM((tm, D), jnp.bfloat16)],
        compiler_params=_cparams(("parallel", "arbitrary")),
        name="inproj",
    )(x2, g, w_big, w_small, head_gains)


def _prep_kernel(small_ref, gb_ref, ikg_ref, ccol_ref, crow_ref, ike_ref, iko_ref, *, tk):
    S = small_ref.shape[0]
    z = small_ref[:, 128:256] + gb_ref[...]
    ls = jnp.minimum(z, 0.0) - jnp.log(1.0 + jnp.exp(-jnp.abs(z)))
    row = lax.broadcasted_iota(jnp.int32, (S, 128), 0)
    c = ls
    sh = 1
    while sh < S:
        c = c + jnp.where(row >= sh, pltpu.roll(c, sh, axis=0), 0.0)
        sh *= 2
    ccol_ref[...] = c
    crow_ref[...] = c.T[0:N_HEADS, :]
    ik = small_ref[:, 0:128]
    ms = jnp.sum(ik * ik, axis=-1, keepdims=True) * (1.0 / 128.0)
    ikn = ik * lax.rsqrt(ms + NORM_EPS) * ikg_ref[...]
    lane = lax.broadcasted_iota(jnp.int32, (S, 128), 1)
    ike = jnp.where(lane < IDX_DIM, ikn, 0.0)
    iko = jnp.where(lane >= IDX_DIM, ikn, 0.0)
    for cidx in range(S // tk):
        ike_ref[cidx] = ike[cidx * tk:(cidx + 1) * tk, :].astype(ike_ref.dtype)
        iko_ref[cidx] = iko[cidx * tk:(cidx + 1) * tk, :].astype(iko_ref.dtype)


def _prep(small3, gate_b128, ikg128, tk=512):
    B, S, _ = small3.shape
    nk = S // tk
    return pl.pallas_call(
        functools.partial(_prep_kernel, tk=tk),
        grid=(B,),
        in_specs=[
            pl.BlockSpec((None, S, SMALL_W), lambda b: (b, 0, 0)),
            pl.BlockSpec((1, 128), lambda b: (0, 0)),
            pl.BlockSpec((1, 128), lambda b: (0, 0)),
        ],
        out_specs=[
            pl.BlockSpec((None, S, 128), lambda b: (b, 0, 0)),
            pl.BlockSpec((None, N_HEADS, S), lambda b: (b, 0, 0)),
            pl.BlockSpec((None, nk, tk, 128), lambda b: (b, 0, 0, 0)),
            pl.BlockSpec((None, nk, tk, 128), lambda b: (b, 0, 0, 0)),
        ],
        out_shape=[
            jax.ShapeDtypeStruct((B, S, 128), jnp.float32),
            jax.ShapeDtypeStruct((B, N_HEADS, S), jnp.float32),
            jax.ShapeDtypeStruct((B, nk, tk, 128), jnp.bfloat16),
            jax.ShapeDtypeStruct((B, nk, tk, 128), jnp.bfloat16),
        ],
        compiler_params=_cparams(("parallel",)),
        name="prep",
    )(small3, gate_b128, ikg128)


def _fox_kernel(q_ref, k_ref, v_ref, ccol_ref, cq_ref, o_ref, acc_ref, ckb_ref, *, tq):
    h = pl.program_id(1)
    i = pl.program_id(2)
    S = k_ref.shape[0]

    @pl.when(i == 0)
    def _():
        lane = lax.broadcasted_iota(jnp.int32, (S, 128), 1)
        col = jnp.sum(jnp.where(lane == h, ccol_ref[...], 0.0), axis=-1, keepdims=True)
        ckb_ref[...] = jnp.broadcast_to(col * LOG2E, (S, 128))

    cq = cq_ref[i] * LOG2E
    q_t = q_ref[...].astype(jnp.float32).T.astype(jnp.bfloat16)

    def update(j, m, l, masked):
        rows = pl.ds(pl.multiple_of(j * tq, tq), tq)
        s = jnp.dot(k_ref[rows, :], q_t, preferred_element_type=jnp.float32)
        t = s * (FOX_SCALE * LOG2E) - jnp.concatenate([ckb_ref[rows, :]] * (tq // 128), axis=1)
        if masked:
            r = lax.broadcasted_iota(jnp.int32, (tq, tq), 0)
            c = lax.broadcasted_iota(jnp.int32, (tq, tq), 1)
            t = jnp.where(r <= c, t, NEG_BIG)
        m_new = jnp.maximum(m, jnp.max(t, axis=0, keepdims=True) + cq)
        p = jnp.exp2(t - (m_new - cq))
        alpha = jnp.exp2(m - m_new)
        l_new = alpha * l + jnp.sum(p, axis=0, keepdims=True)
        pv = lax.dot_general(v_ref[rows, :], p.astype(jnp.bfloat16), (((0,), (0,)), ((), ())),
                             preferred_element_type=jnp.float32)
        acc_ref[...] = alpha * acc_ref[...] + pv
        return m_new, l_new

    acc_ref[...] = jnp.zeros_like(acc_ref)
    m0 = jnp.full((1, tq), NEG_BIG, jnp.float32)
    l0 = jnp.zeros((1, tq), jnp.float32)
    m, l = lax.fori_loop(0, i, lambda j, c: update(j, c[0], c[1], False), (m0, l0))
    m, l = update(i, m, l, True)
    o_ref[...] = (acc_ref[...] / l).T.astype(o_ref.dtype)


def _fox_attention(big3, ccol, ck5, tq=512):
    B, S, _ = big3.shape
    nq = S // tq
    hb = HEAD_WIDTH // HEAD_DIM
    return pl.pallas_call(
        functools.partial(_fox_kernel, tq=tq),
        grid=(B, N_HEADS, nq),
        in_specs=[
            pl.BlockSpec((None, tq, HEAD_DIM), lambda b, h, i: (b, i, COL_FQ * hb + h)),
            pl.BlockSpec((None, S, HEAD_DIM), lambda b, h, i: (b, 0, COL_FK * hb + h)),
            pl.BlockSpec((None, S, HEAD_DIM), lambda b, h, i: (b, 0, COL_FV * hb + h)),
            pl.BlockSpec((None, S, 128), lambda b, h, i: (b, 0, 0)),
            pl.BlockSpec((None, None, nq, 1, tq), lambda b, h, i: (b, h, 0, 0, 0)),
        ],
        out_specs=pl.BlockSpec((None, tq, HEAD_DIM), lambda b, h, i: (b, i, h)),
        out_shape=jax.ShapeDtypeStruct((B, S, HEAD_WIDTH), jnp.bfloat16),
        scratch_shapes=[pltpu.VMEM((HEAD_DIM, tq), jnp.float32), pltpu.VMEM((S, 128), jnp.float32)],
        compiler_params=_cparams(("parallel", "parallel", "arbitrary")),
        name="fox_attn",
    )(big3, big3, big3, ccol, ck5)


CHUNK = 64
IDX_TOPK_MAX = 256
IDX_SCALE = (IDX_DIM ** -0.5) * (IDX_HEADS ** -0.5)
INT_MIN = -2 ** 31


def _indexer_kernel(iq_ref, w_ref, ike_ref, iko_ref, mask_ref, key_ref, iqt_ref, *, tq, tk, topk):
    i = pl.program_id(1)
    nc = key_ref.shape[0]
    q0 = i * tq
    nch = (q0 + tq + tk - 1) // tk
    qpos = q0 + lax.broadcasted_iota(jnp.int32, (1, tq), 1)
    limit = (qpos // CHUNK + 1) * CHUNK
    w_t = w_ref[...].T
    wrows = [w_t[N_HEADS + h:N_HEADS + h + 1, :] for h in range(IDX_HEADS)]
    for p in range(IDX_HEADS // 2):
        iqt_ref[p] = iq_ref[:, p * 128:(p + 1) * 128].astype(jnp.float32).T.astype(iqt_ref.dtype)
    key_pos = lax.broadcasted_iota(jnp.int32, (tk, tq), 0)

    def score_chunk(c, carry):
        acc = jnp.zeros((tk, tq), jnp.float32)
        for p in range(IDX_HEADS // 2):
            de = jnp.dot(ike_ref[c], iqt_ref[p], preferred_element_type=jnp.float32)
            do = jnp.dot(iko_ref[c], iqt_ref[p], preferred_element_type=jnp.float32)
            acc = acc + wrows[2 * p] * jnp.maximum(de, 0.0) + wrows[2 * p + 1] * jnp.maximum(do, 0.0)
        sc = jnp.where(key_pos + c * tk < limit, acc * IDX_SCALE, -jnp.inf)
        bits = pltpu.bitcast(sc, jnp.int32)
        key_ref[c] = bits ^ ((bits >> 31) & 0x7FFFFFFF)
        return carry

    lax.fori_loop(0, nch, score_chunk, 0)

    def search(it, thr):
        cand = thr + (jnp.int32(1) << (31 - it))

        def count_chunk(c, cnt):
            ge = jnp.where(key_ref[c] >= cand, 1, 0)
            return cnt + jnp.sum(ge.reshape(tk // 8, 8, tq), axis=0)

        cnt = lax.fori_loop(0, nch, count_chunk, jnp.zeros((8, tq), jnp.int32))
        total = jnp.sum(cnt, axis=0, keepdims=True)
        return jnp.where(total >= topk, cand, thr)

    thr = lax.fori_loop(0, 32, search, jnp.full((1, tq), INT_MIN, jnp.int32))

    def write_chunk(c, carry):
        admissible = jnp.where(key_pos + c * tk < limit, 0.0, NEG_BIG)
        mask_ref[c] = jnp.where(key_ref[c] >= thr, admissible, NEG_BIG).astype(mask_ref.dtype)
        return carry

    lax.fori_loop(0, nch, write_chunk, 0)

    def fill_chunk(c, carry):
        mask_ref[c] = jnp.full((tk, tq), NEG_BIG, mask_ref.dtype)
        return carry

    lax.fori_loop(nch, nc, fill_chunk, 0)


def _indexer_mask(big3, small3, ike, iko, topk, tq=256):
    B, S, _ = big3.shape
    nc, tk, _ = ike.shape[1:]
    return pl.pallas_call(
        functools.partial(_indexer_kernel, tq=tq, tk=tk, topk=topk),
        grid=(B, S // tq),
        in_specs=[
            pl.BlockSpec((None, tq, HEAD_WIDTH), lambda b, i: (b, i, COL_IQ)),
            pl.BlockSpec((None, tq, 128), lambda b, i: (b, i, 1)),
            pl.BlockSpec((None, nc, tk, 128), lambda b, i: (b, 0, 0, 0)),
            pl.BlockSpec((None, nc, tk, 128), lambda b, i: (b, 0, 0, 0)),
        ],
        out_specs=pl.BlockSpec((None, nc, tk, tq), lambda b, i: (b, 0, 0, i)),
        out_shape=jax.ShapeDtypeStruct((B, nc, tk, S), jnp.bfloat16),
        scratch_shapes=[pltpu.VMEM((nc, tk, tq), jnp.int32),
                        pltpu.VMEM((IDX_HEADS // 2, 128, tq), jnp.bfloat16)],
        compiler_params=_cparams(("parallel", "arbitrary")),
        name="indexer_mask",
    )(big3, small3, ike, iko)


REL_BUCKETS = 32
REL_MAX_DIST = 128
FAR_BUCKET = REL_BUCKETS // 2 - 1


def _t5_bucket(rel):
    half = REL_BUCKETS // 2
    max_exact = half // 2
    ret = jnp.where(rel > 0, half, 0)
    n = jnp.abs(rel)
    nf = jnp.maximum(n, 1).astype(jnp.float32)
    large = max_exact + (jnp.log(nf / max_exact) / math.log(REL_MAX_DIST / max_exact)
                         * (half - max_exact)).astype(jnp.int32)
    large = jnp.minimum(large, half - 1)
    return ret + jnp.where(n < max_exact, n, large)


def _bias_tile_kernel(tab_ref, bucket_ref, out_ref):
    h = pl.program_id(0)
    for d in range(2):
        bk = bucket_ref[d]
        acc = jnp.zeros(bk.shape, jnp.float32)
        for b in range(REL_BUCKETS):
            acc = jnp.where(bk == b, tab_ref[h, b], acc)
        out_ref[d] = acc * LOG2E


def _bias_tiles(rel_bias, tq):
    r = jnp.arange(tq, dtype=jnp.int32)
    rel = r[:, None] - r[None, :]
    buckets = jnp.stack([_t5_bucket(rel), _t5_bucket(rel - tq)])
    return pl.pallas_call(
        _bias_tile_kernel,
        grid=(N_HEADS,),
        in_specs=[
            pl.BlockSpec(memory_space=pltpu.SMEM),
            pl.BlockSpec((2, tq, tq), lambda h: (0, 0, 0)),
        ],
        out_specs=pl.BlockSpec((None, 2, tq, tq), lambda h: (h, 0, 0, 0)),
        out_shape=jax.ShapeDtypeStruct((N_HEADS, 2, tq, tq), jnp.float32),
        compiler_params=_cparams(("parallel",)),
        name="t5_bias_tiles",
    )(rel_bias.T.astype(jnp.float32), buckets)


def _dsa_kernel(tab_ref, q_ref, k_ref, v_ref, mask_ref, bias_ref, o_ref, acc_ref, *, tq):
    i = pl.program_id(1)
    h = pl.program_id(2)
    far_bias = tab_ref[h, FAR_BUCKET] * LOG2E
    q_t = q_ref[...].astype(jnp.float32).T.astype(jnp.bfloat16)

    def update(j, m, l, near):
        rows = pl.ds(pl.multiple_of(j * tq, tq), tq)
        s = jnp.dot(k_ref[rows, :], q_t, preferred_element_type=jnp.float32)
        t = s * (FOX_SCALE * LOG2E) + mask_ref[j].astype(jnp.float32)
        if near is None:
            shift = far_bias
        else:
            t = t + bias_ref[near]
            shift = 0.0
        m_new = jnp.maximum(m, jnp.max(t, axis=0, keepdims=True) + shift)
        p = jnp.exp2(t - (m_new - shift))
        alpha = jnp.exp2(m - m_new)
        l_new = alpha * l + jnp.sum(p, axis=0, keepdims=True)
        pv = lax.dot_general(v_ref[rows, :], p.astype(jnp.bfloat16), (((0,), (0,)), ((), ())),
                             preferred_element_type=jnp.float32)
        acc_ref[...] = alpha * acc_ref[...] + pv
        return m_new, l_new

    acc_ref[...] = jnp.zeros_like(acc_ref)
    m0 = jnp.full((1, tq), NEG_BIG, jnp.float32)
    l0 = jnp.zeros((1, tq), jnp.float32)
    carry = lax.fori_loop(0, i - 1, lambda j, c: update(j, c[0], c[1], None), (m0, l0))
    m, l = lax.cond(i >= 1, lambda c: update(i - 1, c[0], c[1], 1), lambda c: c, carry)
    m, l = update(i, m, l, 0)
    o_ref[...] = (acc_ref[...] / l).T.astype(o_ref.dtype)


def _dsa_attention(big3, mask, bias_near, rel_bias, tq=512):
    B, S, _ = big3.shape
    nq = S // tq
    nc = mask.shape[1]
    assert mask.shape[2] == tq and tq >= REL_MAX_DIST and tq % CHUNK == 0
    hb = HEAD_WIDTH // HEAD_DIM
    return pl.pallas_call(
        functools.partial(_dsa_kernel, tq=tq),
        grid=(B, nq, N_HEADS),
        in_specs=[
            pl.BlockSpec(memory_space=pltpu.SMEM),
            pl.BlockSpec((None, tq, HEAD_DIM), lambda b, i, h: (b, i, COL_DQ * hb + h)),
            pl.BlockSpec((None, S, HEAD_DIM), lambda b, i, h: (b, 0, COL_DK * hb + h)),
            pl.BlockSpec((None, S, HEAD_DIM), lambda b, i, h: (b, 0, COL_DV * hb + h)),
            pl.BlockSpec((None, nc, tq, tq), lambda b, i, h: (b, 0, 0, i)),
            pl.BlockSpec((None, 2, tq, tq), lambda b, i, h: (h, 0, 0, 0)),
        ],
        out_specs=pl.BlockSpec((None, tq, HEAD_DIM), lambda b, i, h: (b, i, h)),
        out_shape=jax.ShapeDtypeStruct((B, S, HEAD_WIDTH), jnp.bfloat16),
        scratch_shapes=[pltpu.VMEM((HEAD_DIM, tq), jnp.float32)],
        compiler_params=_cparams(("parallel", "parallel", "arbitrary")),
        name="dsa_attn",
    )(rel_bias.T.astype(jnp.float32), big3, big3, big3, mask, bias_near)


N_EXPERTS = 32
EXPERT_TOPK = 4
SWIGLU_ALPHA = 1.702
SWIGLU_LIMIT = 7.0


def _outproj_router_kernel(of_ref, od_ref, x_ref, wof_ref, wod_ref, g_ref, rwh_ref, rwl_ref, rb_ref,
                           x1_ref, xn_ref, eidx_ref, rank_ref, gate_ref, cnt_ref, carry_ref, *, tm):
    i = pl.program_id(0)

    @pl.when(i == 0)
    def _():
        carry_ref[...] = jnp.zeros_like(carry_ref)

    x1 = (x_ref[...]
          + jnp.dot(of_ref[...], wof_ref[...], preferred_element_type=jnp.float32)
          + jnp.dot(od_ref[...], wod_ref[...], preferred_element_type=jnp.float32))
    x1_ref[...] = x1
    ms = jnp.mean(x1 * x1, axis=-1, keepdims=True)
    xn = x1 * lax.rsqrt(ms + NORM_EPS) * g_ref[...]
    xn_ref[...] = xn
    xh = xn.astype(jnp.bfloat16)
    xl = (xn - xh.astype(jnp.float32)).astype(jnp.bfloat16)
    nt = (((1,), (1,)), ((), ()))
    logits = (lax.dot_general(rwh_ref[...], xh, nt, preferred_element_type=jnp.float32)
              + lax.dot_general(rwh_ref[...], xl, nt, preferred_element_type=jnp.float32)
              + lax.dot_general(rwl_ref[...], xh, nt, preferred_element_type=jnp.float32)
              + rb_ref[...])
    eio = lax.broadcasted_iota(jnp.int32, (N_EXPERTS, tm), 0)
    work = logits
    vals, idxs = [], []
    multihot = jnp.zeros((N_EXPERTS, tm), jnp.float32)
    for _ in range(EXPERT_TOPK):
        mx = jnp.max(work, axis=0, keepdims=True)
        ix = jnp.min(jnp.where(work == mx, eio, N_EXPERTS), axis=0, keepdims=True)
        hit = eio == ix
        multihot = jnp.where(hit, 1.0, multihot)
        work = jnp.where(hit, -jnp.inf, work)
        vals.append(mx)
        idxs.append(ix)
    ex = [jnp.exp(v - vals[0]) for v in vals]
    den = ex[0] + ex[1] + ex[2] + ex[3]
    a = lax.broadcasted_iota(jnp.int32, (tm, tm), 0)
    b = lax.broadcasted_iota(jnp.int32, (tm, tm), 1)
    upper = jnp.where(a < b, 1.0, 0.0).astype(jnp.bfloat16)
    before = jnp.dot(multihot.astype(jnp.bfloat16), upper,
                     preferred_element_type=jnp.float32) + carry_ref[:, 0:1]
    for r in range(EXPERT_TOPK):
        eidx_ref[r:r + 1, :] = idxs[r]
        gate_ref[r:r + 1, :] = ex[r] / den
        rank_ref[r:r + 1, :] = jnp.sum(jnp.where(eio == idxs[r], before, 0.0),
                                       axis=0, keepdims=True).astype(jnp.int32)
    carry_ref[...] = carry_ref[...] + jnp.sum(multihot, axis=1, keepdims=True)
    cnt_ref[...] = carry_ref[...].astype(jnp.int32)


def _outproj_router(o_fox, o_dsa, x2, wo_f, wo_d, g, rw_hi, rw_lo, rb, tm=512):
    T, D = x2.shape
    hw = o_fox.shape[1]
    full = lambda shape: pl.BlockSpec(shape, lambda i: tuple(0 for _ in shape))
    return pl.pallas_call(
        functools.partial(_outproj_router_kernel, tm=tm),
        grid=(T // tm,),
        in_specs=[
            pl.BlockSpec((tm, hw), lambda i: (i, 0)),
            pl.BlockSpec((tm, hw), lambda i: (i, 0)),
            pl.BlockSpec((tm, D), lambda i: (i, 0)),
            full((hw, D)), full((hw, D)), full((1, D)),
            full((N_EXPERTS, D)), full((N_EXPERTS, D)), full((N_EXPERTS, 1)),
        ],
        out_specs=[
            pl.BlockSpec((tm, D), lambda i: (i, 0)),
            pl.BlockSpec((tm, D), lambda i: (i, 0)),
            pl.BlockSpec((EXPERT_TOPK, tm), lambda i: (0, i)),
            pl.BlockSpec((EXPERT_TOPK, tm), lambda i: (0, i)),
            pl.BlockSpec((EXPERT_TOPK, tm), lambda i: (0, i)),
            full((N_EXPERTS, 128)),
        ],
        out_shape=[
            jax.ShapeDtypeStruct((T, D), jnp.float32),
            jax.ShapeDtypeStruct((T, D), jnp.float32),
            jax.ShapeDtypeStruct((EXPERT_TOPK, T), jnp.int32),
            jax.ShapeDtypeStruct((EXPERT_TOPK, T), jnp.int32),
            jax.ShapeDtypeStruct((EXPERT_TOPK, T), jnp.float32),
            jax.ShapeDtypeStruct((N_EXPERTS, 128), jnp.int32),
        ],
        scratch_shapes=[pltpu.VMEM((N_EXPERTS, 128), jnp.float32)],
        compiler_params=_cparams(("arbitrary",)),
        name="outproj_router",
    )(o_fox, o_dsa, x2, wo_f, wo_d, g, rw_hi, rw_lo, rb)


ROW_UNROLL = 4


def _for_each_row_copy(tt, fn):
    def body(g, c):
        for u in range(ROW_UNROLL):
            for k in range(EXPERT_TOPK):
                fn(g * ROW_UNROLL + u, k)
        return c

    lax.fori_loop(0, tt // ROW_UNROLL, body, 0)


def _dispatch_kernel(dest_ref, xn_ref, xg_in_ref, xg_ref, sem, *, tt):
    del xg_in_ref

    def row_copy(t, k):
        return pltpu.make_async_copy(xn_ref.at[pl.ds(t, 1)], xg_ref.at[pl.ds(dest_ref[k, t], 1)], sem)

    _for_each_row_copy(tt, lambda t, k: row_copy(t, k).start())
    _for_each_row_copy(tt, lambda t, k: row_copy(t, k).wait())


def _dispatch(dest, xn, n_rows, tt=256):
    T, D = xn.shape
    xg0 = jnp.zeros((n_rows, D), xn.dtype)
    return pl.pallas_call(
        functools.partial(_dispatch_kernel, tt=tt),
        grid=(T // tt,),
        in_specs=[
            pl.BlockSpec((EXPERT_TOPK, tt), lambda i: (0, i), memory_space=pltpu.SMEM),
            pl.BlockSpec((tt, D), lambda i: (i, 0)),
            pl.BlockSpec(memory_space=pl.ANY),
        ],
        out_specs=pl.BlockSpec(memory_space=pl.ANY),
        out_shape=jax.ShapeDtypeStruct((n_rows, D), xn.dtype),
        scratch_shapes=[pltpu.SemaphoreType.DMA(())],
        input_output_aliases={2: 0},
        compiler_params=_cparams(("arbitrary",)),
        name="moe_dispatch",
    )(dest, xn, xg0)


MXU_WIDTH = 256


def _w1_split_kernel(w_ref, g_ref, l_ref):
    half = MXU_WIDTH // 2
    r = lax.broadcasted_iota(jnp.int32, (MXU_WIDTH, MXU_WIDTH), 0)
    c = lax.broadcasted_iota(jnp.int32, (MXU_WIDTH, MXU_WIDTH), 1)
    src = jnp.where(c < half, 2 * c, 2 * (c - half) + 1)
    perm = jnp.where(r == src, 1.0, 0.0).astype(jnp.bfloat16)
    for k in range(w_ref.shape[1] // MXU_WIDTH):
        wk = w_ref[:, k * MXU_WIDTH:(k + 1) * MXU_WIDTH].astype(jnp.bfloat16)
        out = jnp.dot(wk, perm, preferred_element_type=jnp.float32)
        g_ref[:, k * half:(k + 1) * half] = out[:, :half].astype(g_ref.dtype)
        l_ref[:, k * half:(k + 1) * half] = out[:, half:].astype(l_ref.dtype)


def _w1_split(w1, tn=512):
    E, D, F2 = w1.shape
    F = F2 // 2
    return pl.pallas_call(
        _w1_split_kernel,
        grid=(E, F // tn),
        in_specs=[pl.BlockSpec((None, D, 2 * tn), lambda e, j: (e, 0, j))],
        out_specs=[pl.BlockSpec((None, D, tn), lambda e, j: (e, 0, j)),
                   pl.BlockSpec((None, D, tn), lambda e, j: (e, 0, j))],
        out_shape=[jax.ShapeDtypeStruct((E, D, F), jnp.bfloat16),
                   jax.ShapeDtypeStruct((E, D, F), jnp.bfloat16)],
        compiler_params=_cparams(("parallel", "parallel")),
        name="w1_split",
    )(w1)


def _expert_kernel(te_ref, nu_ref, x_ref, w1g_ref, w1l_ref, b1g_ref, b1l_ref, w2_ref, b2_ref,
                   y_ref, xb_ref, acc_ref):
    i = pl.program_id(0)
    f = pl.program_id(1)
    nf = pl.num_programs(1)

    @pl.when(i < nu_ref[0])
    def _():
        @pl.when(f == 0)
        def _():
            xb_ref[...] = x_ref[...].astype(xb_ref.dtype)
            acc_ref[...] = jnp.zeros_like(acc_ref)

        xb = xb_ref[...]
        glu = jnp.dot(xb, w1g_ref[...], preferred_element_type=jnp.float32) + b1g_ref[...]
        lin = jnp.dot(xb, w1l_ref[...], preferred_element_type=jnp.float32) + b1l_ref[...]
        glu = jnp.minimum(glu, SWIGLU_LIMIT)
        lin = jnp.clip(lin, -SWIGLU_LIMIT, SWIGLU_LIMIT)
        act = glu * (1.0 / (1.0 + jnp.exp(-SWIGLU_ALPHA * glu))) * (lin + 1.0)
        acc_ref[...] += jnp.dot(act.astype(jnp.bfloat16), w2_ref[...].astype(jnp.bfloat16),
                                preferred_element_type=jnp.float32)

        @pl.when(f == nf - 1)
        def _():
            y_ref[...] = acc_ref[...] + b2_ref[...]

    @pl.when((i >= nu_ref[0]) & (f == nf - 1))
    def _():
        y_ref[...] = jnp.zeros_like(y_ref)


def _experts(tile_expert, n_used, xg, w1g, w1l, b1g, b1l, w2b, b2, tme, tf=512):
    P, D = xg.shape
    F = w2b.shape[1]
    nf = F // tf
    row = lambda i, f, te, nu: (jnp.minimum(i, nu[0] - 1), 0)
    exp = lambda i, te, nu: te[jnp.minimum(i, nu[0] - 1)]
    fblk = lambda i, f, nu: jnp.where(i < nu[0], f, nf - 1)
    grid_spec = pltpu.PrefetchScalarGridSpec(
        num_scalar_prefetch=2,
        grid=(P // tme, nf),
        in_specs=[
            pl.BlockSpec((tme, D), row),
            pl.BlockSpec((None, D, tf), lambda i, f, te, nu: (exp(i, te, nu), 0, fblk(i, f, nu))),
            pl.BlockSpec((None, D, tf), lambda i, f, te, nu: (exp(i, te, nu), 0, fblk(i, f, nu))),
            pl.BlockSpec((None, 1, tf), lambda i, f, te, nu: (exp(i, te, nu), 0, fblk(i, f, nu))),
            pl.BlockSpec((None, 1, tf), lambda i, f, te, nu: (exp(i, te, nu), 0, fblk(i, f, nu))),
            pl.BlockSpec((None, tf, D), lambda i, f, te, nu: (exp(i, te, nu), fblk(i, f, nu), 0)),
            pl.BlockSpec((None, 1, D), lambda i, f, te, nu: (exp(i, te, nu), 0, 0)),
        ],
        out_specs=pl.BlockSpec((tme, D), lambda i, f, te, nu: (i, 0)),
        scratch_shapes=[pltpu.VMEM((tme, D), jnp.bfloat16), pltpu.VMEM((tme, D), jnp.float32)],
    )
    return pl.pallas_call(
        _expert_kernel,
        grid_spec=grid_spec,
        out_shape=jax.ShapeDtypeStruct((P, D), jnp.float32),
        compiler_params=_cparams(("arbitrary", "arbitrary")),
        name="moe_experts",
    )(tile_expert, n_used, xg, w1g, w1l, b1g, b1l, w2b, b2)


def _combine_kernel(dest_ref, x1_ref, gt_ref, y_ref, o_ref, buf_ref, sem, *, tt):
    def row_copy(t, k):
        return pltpu.make_async_copy(y_ref.at[pl.ds(dest_ref[k, t], 1)],
                                     buf_ref.at[k, pl.ds(t, 1)], sem)

    _for_each_row_copy(tt, lambda t, k: row_copy(t, k).start())
    _for_each_row_copy(tt, lambda t, k: row_copy(t, k).wait())
    gt = gt_ref[...]
    out = x1_ref[...]
    for k in range(EXPERT_TOPK):
        out = out + gt[:, k:k + 1] * buf_ref[k]
    o_ref[...] = out


def _combine(dest, x1, gates_t, yrows, tt=128):
    T, D = x1.shape
    return pl.pallas_call(
        functools.partial(_combine_kernel, tt=tt),
        grid=(T // tt,),
        in_specs=[
            pl.BlockSpec((EXPERT_TOPK, tt), lambda i: (0, i), memory_space=pltpu.SMEM),
            pl.BlockSpec((tt, D), lambda i: (i, 0)),
            pl.BlockSpec((tt, 8), lambda i: (i, 0)),
            pl.BlockSpec(memory_space=pl.ANY),
        ],
        out_specs=pl.BlockSpec((tt, D), lambda i: (i, 0)),
        out_shape=jax.ShapeDtypeStruct((T, D), x1.dtype),
        scratch_shapes=[pltpu.VMEM((EXPERT_TOPK, tt, D), jnp.float32), pltpu.SemaphoreType.DMA(())],
        compiler_params=_cparams(("arbitrary",)),
        name="moe_combine",
    )(dest, x1, gates_t, yrows)


def kernel_parts(x, attn_norm_g, w_in, fox_gate_b, fox_q_g, fox_k_g, dsa_q_g, dsa_k_g, idx_k_g,
                 rel_bias, w_out, ffn_norm_g, router_w, router_b, w1, b1, w2, b2):
    B, S, D = x.shape
    T = B * S
    l = 0
    f32 = jnp.float32
    wi = w_in[l]
    o = 0
    cols = {}
    for name, width in (("fq", HEAD_WIDTH), ("fk", HEAD_WIDTH), ("fv", HEAD_WIDTH), ("ff", N_HEADS),
                        ("dq", HEAD_WIDTH), ("dk", HEAD_WIDTH), ("dv", HEAD_WIDTH),
                        ("iq", IDX_HEADS * IDX_DIM), ("ik", IDX_DIM), ("iw", IDX_HEADS)):
        cols[name] = wi[:, o:o + width]
        o += width
    w_big = jnp.concatenate([cols[n] for n in ("fq", "fk", "dq", "dk", "fv", "dv", "iq")],
                            axis=1).astype(jnp.bfloat16)
    pad = jnp.zeros((D, SMALL_W - 2 * IDX_DIM - N_HEADS - IDX_HEADS), f32)
    w_small = jnp.concatenate([cols["ik"], cols["ik"], cols["ff"], cols["iw"], pad],
                              axis=1).astype(jnp.bfloat16)
    head_gains = jnp.stack([fox_q_g[l], fox_k_g[l], dsa_q_g[l], dsa_k_g[l]]).reshape(4, 1, HEAD_DIM)
    gate_b128 = jnp.zeros((1, 128), f32).at[0, :N_HEADS].set(fox_gate_b[l])
    ikg128 = jnp.concatenate([idx_k_g[l], idx_k_g[l]]).reshape(1, 128)

    x2 = x.reshape(T, D)
    big, small = _inproj(x2, attn_norm_g[l].reshape(1, D), w_big, w_small, head_gains)
    big3 = big.reshape(B, S, N_BIG_TILES * HEAD_WIDTH)
    small3 = small.reshape(B, S, SMALL_W)
    ccol, crow, ike, iko = _prep(small3, gate_b128, ikg128)
    tq = 512
    ck5 = crow.reshape(B, N_HEADS, S // tq, 1, tq)
    o_fox = _fox_attention(big3, ccol, ck5, tq=tq)
    topk = min(IDX_TOPK_MAX, S // 4)
    mask = _indexer_mask(big3, small3, ike, iko, topk)
    bias_near = _bias_tiles(rel_bias, tq)
    o_dsa = _dsa_attention(big3, mask, bias_near, rel_bias, tq=tq)
    bf16 = jnp.bfloat16
    wo = w_out[l].astype(bf16)
    rw = router_w[l].T
    rw_hi = rw.astype(bf16)
    rw_lo = (rw - rw_hi.astype(f32)).astype(bf16)
    x1, xn, eidx, rank, gates, cnt = _outproj_router(
        o_fox.reshape(T, HEAD_WIDTH), o_dsa.reshape(T, HEAD_WIDTH), x2, wo[:HEAD_WIDTH], wo[HEAD_WIDTH:],
        ffn_norm_g[l].reshape(1, D), rw_hi, rw_lo, router_b[l].reshape(N_EXPERTS, 1))
    tme = 512
    n_rows = T * EXPERT_TOPK + N_EXPERTS * tme
    n_tiles = n_rows // tme
    counts = cnt[:, 0]
    tiles_e = (counts + tme - 1) // tme
    tile_end = jnp.cumsum(tiles_e)
    pad_start = (tile_end - tiles_e) * tme
    e_ids = jnp.arange(N_EXPERTS, dtype=jnp.int32)
    dest = rank + jnp.sum(jnp.where(eidx[None] == e_ids[:, None, None], pad_start[:, None, None], 0),
                          axis=0)
    tile_expert = jnp.minimum(
        jnp.sum(tile_end[None, :] <= jnp.arange(n_tiles, dtype=jnp.int32)[:, None], axis=1),
        N_EXPERTS - 1).astype(jnp.int32)
    n_used = tile_end[-1:].astype(jnp.int32)
    xg = _dispatch(dest, xn, n_rows)
    F = w2.shape[2]
    w1g, w1l = _w1_split(w1[l])
    b1g = b1[l][:, 0::2].reshape(N_EXPERTS, 1, F)
    b1l = b1[l][:, 1::2].reshape(N_EXPERTS, 1, F)
    yrows = _experts(tile_expert, n_used, xg, w1g, w1l, b1g, b1l, w2[l],
                     b2[l].reshape(N_EXPERTS, 1, D), tme)
    gates_t = jnp.pad(gates.T, ((0, 0), (0, 8 - EXPERT_TOPK)))
    out = _combine(dest, x1, gates_t, yrows)
    return dict(o_fox=o_fox, o_dsa=o_dsa, mask=mask, x1=x1, out=out.reshape(B, S, D))


def kernel(x, attn_norm_g, w_in, fox_gate_b, fox_q_g, fox_k_g, dsa_q_g, dsa_k_g, idx_k_g,
           rel_bias, w_out, ffn_norm_g, router_w, router_b, w1, b1, w2, b2):
    return kernel_parts(x, attn_norm_g, w_in, fox_gate_b, fox_q_g, fox_k_g, dsa_q_g, dsa_k_g, idx_k_g,
                        rel_bias, w_out, ffn_norm_g, router_w, router_b, w1, b1, w2, b2)["out"]
```

```python
import functools
import math

import jax
import jax.numpy as jnp
import numpy as np
from jax import lax
from jax.experimental import pallas as pl
from jax.experimental.pallas import tpu as pltpu

HEAD_DIM = 128
N_HEADS = 8
HEAD_WIDTH = N_HEADS * HEAD_DIM
IDX_HEADS = 16
IDX_DIM = 64
NORM_EPS = 1e-6
FOX_SCALE = HEAD_DIM ** -0.5
NEG_BIG = -1e30
LOG2E = math.log2(math.e)

COL_FQ, COL_FK, COL_DQ, COL_DK, COL_FV, COL_DV, COL_IQ = range(7)
N_BIG_TILES = 7
N_NORM_TILES = 4
SMALL_W = 256

VMEM_LIMIT = 56 * 1024 * 1024


def _cparams(sem):
    return pltpu.CompilerParams(dimension_semantics=sem, vmem_limit_bytes=VMEM_LIMIT)


def _inproj_kernel(x_ref, g_ref, wb_ref, ws_ref, hg_ref, big_ref, small_ref, xn_ref):
    j = pl.program_id(1)

    @pl.when(j == 0)
    def _():
        x = x_ref[...]
        ms = jnp.mean(x * x, axis=-1, keepdims=True)
        xn = (x * lax.rsqrt(ms + NORM_EPS) * g_ref[...]).astype(jnp.bfloat16)
        xn_ref[...] = xn
        small_ref[...] = jnp.dot(xn, ws_ref[...], preferred_element_type=jnp.float32)

    @pl.when(j < N_NORM_TILES)
    def _():
        hg = hg_ref[...]
        for c0 in range(0, HEAD_WIDTH, 2 * HEAD_DIM):
            acc = jnp.dot(xn_ref[...], wb_ref[:, c0:c0 + 2 * HEAD_DIM],
                          preferred_element_type=jnp.float32)
            for c in (0, HEAD_DIM):
                t = acc[:, c:c + HEAD_DIM]
                ms = jnp.mean(t * t, axis=-1, keepdims=True)
                big_ref[:, c0 + c:c0 + c + HEAD_DIM] = (
                    t * lax.rsqrt(ms + NORM_EPS) * hg).astype(big_ref.dtype)

    @pl.when(j >= N_NORM_TILES)
    def _():
        big_ref[...] = jnp.dot(xn_ref[...], wb_ref[...],
                               preferred_element_type=jnp.float32).astype(big_ref.dtype)


def _inproj(x2, g, w_big, w_small, head_gains, tm=1024):
    T, D = x2.shape
    return pl.pallas_call(
        _inproj_kernel,
        grid=(T // tm, N_BIG_TILES),
        in_specs=[
            pl.BlockSpec((tm, D), lambda i, j: (i, 0)),
            pl.BlockSpec((1, D), lambda i, j: (0, 0)),
            pl.BlockSpec((D, HEAD_WIDTH), lambda i, j: (0, j)),
            pl.BlockSpec((D, SMALL_W), lambda i, j: (0, 0)),
            pl.BlockSpec((None, 1, HEAD_DIM), lambda i, j: (jnp.minimum(j, N_NORM_TILES - 1), 0, 0)),
        ],
        out_specs=[
            pl.BlockSpec((tm, HEAD_WIDTH), lambda i, j: (i, j)),
            pl.BlockSpec((tm, SMALL_W), lambda i, j: (i, 0)),
        ],
        out_shape=[
            jax.ShapeDtypeStruct((T, N_BIG_TILES * HEAD_WIDTH), jnp.bfloat16),
            jax.ShapeDtypeStruct((T, SMALL_W), jnp.float32),
        ],
        scratch_shapes=[pltpu.VMEM((tm, D), jnp.bfloat16)],
        compiler_params=_cparams(("parallel", "arbitrary")),
        name="inproj",
    )(x2, g, w_big, w_small, head_gains)


def _prep_kernel(small_ref, gb_ref, ikg_ref, ccol_ref, crow_ref, ike_ref, iko_ref, *, tk):
    S = small_ref.shape[0]
    z = small_ref[:, 128:256] + gb_ref[...]
    ls = jnp.minimum(z, 0.0) - jnp.log(1.0 + jnp.exp(-jnp.abs(z)))
    row = lax.broadcasted_iota(jnp.int32, (S, 128), 0)
    c = ls
    sh = 1
    while sh < S:
        c = c + jnp.where(row >= sh, pltpu.roll(c, sh, axis=0), 0.0)
        sh *= 2
    ccol_ref[...] = c
    crow_ref[...] = c.T[0:N_HEADS, :]
    ik = small_ref[:, 0:128]
    ms = jnp.sum(ik * ik, axis=-1, keepdims=True) * (1.0 / 128.0)
    ikn = ik * lax.rsqrt(ms + NORM_EPS) * ikg_ref[...]
    lane = lax.broadcasted_iota(jnp.int32, (S, 128), 1)
    ike = jnp.where(lane < IDX_DIM, ikn, 0.0)
    iko = jnp.where(lane >= IDX_DIM, ikn, 0.0)
    for cidx in range(S // tk):
        ike_ref[cidx] = ike[cidx * tk:(cidx + 1) * tk, :].astype(ike_ref.dtype)
        iko_ref[cidx] = iko[cidx * tk:(cidx + 1) * tk, :].astype(iko_ref.dtype)


def _prep(small3, gate_b128, ikg128, tk=512):
    B, S, _ = small3.shape
    nk = S // tk
    return pl.pallas_call(
        functools.partial(_prep_kernel, tk=tk),
        grid=(B,),
        in_specs=[
            pl.BlockSpec((None, S, SMALL_W), lambda b: (b, 0, 0)),
            pl.BlockSpec((1, 128), lambda b: (0, 0)),
            pl.BlockSpec((1, 128), lambda b: (0, 0)),
        ],
        out_specs=[
            pl.BlockSpec((None, S, 128), lambda b: (b, 0, 0)),
            pl.BlockSpec((None, N_HEADS, S), lambda b: (b, 0, 0)),
            pl.BlockSpec((None, nk, tk, 128), lambda b: (b, 0, 0, 0)),
            pl.BlockSpec((None, nk, tk, 128), lambda b: (b, 0, 0, 0)),
        ],
        out_shape=[
            jax.ShapeDtypeStruct((B, S, 128), jnp.float32),
            jax.ShapeDtypeStruct((B, N_HEADS, S), jnp.float32),
            jax.ShapeDtypeStruct((B, nk, tk, 128), jnp.bfloat16),
            jax.ShapeDtypeStruct((B, nk, tk, 128), jnp.bfloat16),
        ],
        compiler_params=_cparams(("parallel",)),
        name="prep",
    )(small3, gate_b128, ikg128)


MXU_WIDTH = 256


def _w1_split_slab(w_ref, g_ref, l_ref):
    half = MXU_WIDTH // 2
    r = lax.broadcasted_iota(jnp.int32, (MXU_WIDTH, MXU_WIDTH), 0)
    c = lax.broadcasted_iota(jnp.int32, (MXU_WIDTH, MXU_WIDTH), 1)
    src = jnp.where(c < half, 2 * c, 2 * (c - half) + 1)
    perm = jnp.where(r == src, 1.0, 0.0).astype(jnp.bfloat16)
    for n in range(w_ref.shape[0]):
        for k in range(w_ref.shape[2] // MXU_WIDTH):
            wk = w_ref[n, :, k * MXU_WIDTH:(k + 1) * MXU_WIDTH].astype(jnp.bfloat16)
            out = jnp.dot(wk, perm, preferred_element_type=jnp.float32)
            g_ref[n, :, k * half:(k + 1) * half] = out[:, :half].astype(g_ref.dtype)
            l_ref[n, :, k * half:(k + 1) * half] = out[:, half:].astype(l_ref.dtype)


def _w1_slab_plan(w1_shape, steps):
    E, _, F2 = w1_shape
    cols = E * F2 // steps
    assert cols * steps == E * F2 and cols % MXU_WIDTH == 0
    if cols <= F2:
        assert F2 % cols == 0
        return 1, cols
    assert cols % F2 == 0
    return cols // F2, F2


def _fox_kernel(q_ref, k_ref, v_ref, ccol_ref, cq_ref, w1_ref, o_ref, w1g_ref, w1l_ref,
                acc_ref, ckb_ref, *, tq):
    h = pl.program_id(1)
    i = pl.program_id(2)
    S = k_ref.shape[0]

    @pl.when(i == 0)
    def _():
        lane = lax.broadcasted_iota(jnp.int32, (S, 128), 1)
        col = jnp.sum(jnp.where(lane == h, ccol_ref[...], 0.0), axis=-1, keepdims=True)
        ckb_ref[...] = jnp.broadcast_to(col * LOG2E, (S, 128))

    cq = cq_ref[i] * LOG2E
    q_t = q_ref[...].astype(jnp.float32).T.astype(jnp.bfloat16)

    def update(j, m, l, masked):
        rows = pl.ds(pl.multiple_of(j * tq, tq), tq)
        s = jnp.dot(k_ref[rows, :], q_t, preferred_element_type=jnp.float32)
        t = s * (FOX_SCALE * LOG2E) - jnp.concatenate([ckb_ref[rows, :]] * (tq // 128), axis=1)
        if masked:
            r = lax.broadcasted_iota(jnp.int32, (tq, tq), 0)
            c = lax.broadcasted_iota(jnp.int32, (tq, tq), 1)
            t = jnp.where(r <= c, t, NEG_BIG)
        m_new = jnp.maximum(m, jnp.max(t, axis=0, keepdims=True) + cq)
        p = jnp.exp2(t - (m_new - cq))
        alpha = jnp.exp2(m - m_new)
        l_new = alpha * l + jnp.sum(p, axis=0, keepdims=True)
        pv = lax.dot_general(v_ref[rows, :], p.astype(jnp.bfloat16), (((0,), (0,)), ((), ())),
                             preferred_element_type=jnp.float32)
        acc_ref[...] = alpha * acc_ref[...] + pv
        return m_new, l_new

    acc_ref[...] = jnp.zeros_like(acc_ref)
    m0 = jnp.full((1, tq), NEG_BIG, jnp.float32)
    l0 = jnp.zeros((1, tq), jnp.float32)
    m, l = lax.fori_loop(0, i, lambda j, c: update(j, c[0], c[1], False), (m0, l0))
    m, l = update(i, m, l, True)
    o_ref[...] = (acc_ref[...] / l).T.astype(o_ref.dtype)
    _w1_split_slab(w1_ref, w1g_ref, w1l_ref)


def _fox_attention(big3, ccol, ck5, w1, tq=512):
    B, S, _ = big3.shape
    nq = S // tq
    hb = HEAD_WIDTH // HEAD_DIM
    E, D, F2 = w1.shape
    ne, cw = _w1_slab_plan(w1.shape, B * N_HEADS * nq)
    slabs_per_expert = F2 // cw

    def slab(b, h, i):
        s = (b * N_HEADS + h) * nq + i
        return (s // slabs_per_expert, 0, s % slabs_per_expert) if ne == 1 else (s, 0, 0)

    return pl.pallas_call(
        functools.partial(_fox_kernel, tq=tq),
        grid=(B, N_HEADS, nq),
        in_specs=[
            pl.BlockSpec((None, tq, HEAD_DIM), lambda b, h, i: (b, i, COL_FQ * hb + h)),
            pl.BlockSpec((None, S, HEAD_DIM), lambda b, h, i: (b, 0, COL_FK * hb + h)),
            pl.BlockSpec((None, S, HEAD_DIM), lambda b, h, i: (b, 0, COL_FV * hb + h)),
            pl.BlockSpec((None, S, 128), lambda b, h, i: (b, 0, 0)),
            pl.BlockSpec((None, None, nq, 1, tq), lambda b, h, i: (b, h, 0, 0, 0)),
            pl.BlockSpec((ne, D, cw), slab),
        ],
        out_specs=[
            pl.BlockSpec((None, tq, HEAD_DIM), lambda b, h, i: (b, i, h)),
            pl.BlockSpec((ne, D, cw // 2), slab),
            pl.BlockSpec((ne, D, cw // 2), slab),
        ],
        out_shape=[
            jax.ShapeDtypeStruct((B, S, HEAD_WIDTH), jnp.bfloat16),
            jax.ShapeDtypeStruct((E, D, F2 // 2), jnp.bfloat16),
            jax.ShapeDtypeStruct((E, D, F2 // 2), jnp.bfloat16),
        ],
        scratch_shapes=[pltpu.VMEM((HEAD_DIM, tq), jnp.float32), pltpu.VMEM((S, 128), jnp.float32)],
        compiler_params=_cparams(("parallel", "parallel", "arbitrary")),
        name="fox_attn",
    )(big3, big3, big3, ccol, ck5, w1)


CHUNK = 64
IDX_TOPK_MAX = 256
IDX_SCALE = (IDX_DIM ** -0.5) * (IDX_HEADS ** -0.5)
INT_MIN = -2 ** 31


def _indexer_kernel(iq_ref, w_ref, ike_ref, iko_ref, w2_ref, mask_ref, w2b_ref, key_ref, iqt_ref,
                    *, tq, tk, topk):
    w2b_ref[...] = w2_ref[...].astype(w2b_ref.dtype)
    i = pl.program_id(1)
    nc = key_ref.shape[0]
    q0 = i * tq
    nch = (q0 + tq + tk - 1) // tk
    qpos = q0 + lax.broadcasted_iota(jnp.int32, (1, tq), 1)
    limit = (qpos // CHUNK + 1) * CHUNK
    w_t = w_ref[...].T
    wrows = [w_t[N_HEADS + h:N_HEADS + h + 1, :] for h in range(IDX_HEADS)]
    for p in range(IDX_HEADS // 2):
        iqt_ref[p] = iq_ref[:, p * 128:(p + 1) * 128].astype(jnp.float32).T.astype(iqt_ref.dtype)
    key_pos = lax.broadcasted_iota(jnp.int32, (tk, tq), 0)

    def score_chunk(c, carry):
        acc = jnp.zeros((tk, tq), jnp.float32)
        for p in range(IDX_HEADS // 2):
            de = jnp.dot(ike_ref[c], iqt_ref[p], preferred_element_type=jnp.float32)
            do = jnp.dot(iko_ref[c], iqt_ref[p], preferred_element_type=jnp.float32)
            acc = acc + wrows[2 * p] * jnp.maximum(de, 0.0) + wrows[2 * p + 1] * jnp.maximum(do, 0.0)
        sc = jnp.where(key_pos + c * tk < limit, acc * IDX_SCALE, -jnp.inf)
        bits = pltpu.bitcast(sc, jnp.int32)
        key_ref[c] = bits ^ ((bits >> 31) & 0x7FFFFFFF)
        return carry

    lax.fori_loop(0, nch, score_chunk, 0)

    def search(it, thr):
        cand = thr + (jnp.int32(1) << (31 - it))

        def count_chunk(c, cnt):
            ge = jnp.where(key_ref[c] >= cand, 1, 0)
            return cnt + jnp.sum(ge.reshape(tk // 8, 8, tq), axis=0)

        cnt = lax.fori_loop(0, nch, count_chunk, jnp.zeros((8, tq), jnp.int32))
        total = jnp.sum(cnt, axis=0, keepdims=True)
        return jnp.where(total >= topk, cand, thr)

    thr = lax.fori_loop(0, 32, search, jnp.full((1, tq), INT_MIN, jnp.int32))

    def write_chunk(c, carry):
        admissible = jnp.where(key_pos + c * tk < limit, 0.0, NEG_BIG)
        mask_ref[c] = jnp.where(key_ref[c] >= thr, admissible, NEG_BIG).astype(mask_ref.dtype)
        return carry

    lax.fori_loop(0, nch, write_chunk, 0)

    def fill_chunk(c, carry):
        mask_ref[c] = jnp.full((tk, tq), NEG_BIG, mask_ref.dtype)
        return carry

    lax.fori_loop(nch, nc, fill_chunk, 0)


def _indexer_mask(big3, small3, ike, iko, topk, w2, tq=256):
    B, S, _ = big3.shape
    nc, tk, _ = ike.shape[1:]
    nq = S // tq
    w2_2d = w2.reshape(-1, w2.shape[-1])
    wrows = w2_2d.shape[0] // (B * nq)
    assert wrows * B * nq == w2_2d.shape[0] and wrows % 16 == 0
    mask, w2b = pl.pallas_call(
        functools.partial(_indexer_kernel, tq=tq, tk=tk, topk=topk),
        grid=(B, nq),
        in_specs=[
            pl.BlockSpec((None, tq, HEAD_WIDTH), lambda b, i: (b, i, COL_IQ)),
            pl.BlockSpec((None, tq, 128), lambda b, i: (b, i, 1)),
            pl.BlockSpec((None, nc, tk, 128), lambda b, i: (b, 0, 0, 0)),
            pl.BlockSpec((None, nc, tk, 128), lambda b, i: (b, 0, 0, 0)),
            pl.BlockSpec((wrows, w2_2d.shape[1]), lambda b, i: (b * nq + i, 0)),
        ],
        out_specs=[
            pl.BlockSpec((None, nc, tk, tq), lambda b, i: (b, 0, 0, i)),
            pl.BlockSpec((wrows, w2_2d.shape[1]), lambda b, i: (b * nq + i, 0)),
        ],
        out_shape=[
            jax.ShapeDtypeStruct((B, nc, tk, S), jnp.bfloat16),
            jax.ShapeDtypeStruct(w2_2d.shape, jnp.bfloat16),
        ],
        scratch_shapes=[pltpu.VMEM((nc, tk, tq), jnp.int32),
                        pltpu.VMEM((IDX_HEADS // 2, 128, tq), jnp.bfloat16)],
        compiler_params=_cparams(("parallel", "arbitrary")),
        name="indexer_mask",
    )(big3, small3, ike, iko, w2_2d)
    return mask, w2b.reshape(w2.shape)


REL_BUCKETS = 32
REL_MAX_DIST = 128
FAR_BUCKET = REL_BUCKETS // 2 - 1


def _t5_bucket(rel):
    half = REL_BUCKETS // 2
    max_exact = half // 2
    ret = jnp.where(rel > 0, half, 0)
    n = jnp.abs(rel)
    nf = jnp.maximum(n, 1).astype(jnp.float32)
    large = max_exact + (jnp.log(nf / max_exact) / math.log(REL_MAX_DIST / max_exact)
                         * (half - max_exact)).astype(jnp.int32)
    large = jnp.minimum(large, half - 1)
    return ret + jnp.where(n < max_exact, n, large)


def _bias_tile_kernel(tab_ref, bucket_ref, out_ref):
    h = pl.program_id(0)
    for d in range(2):
        bk = bucket_ref[d]
        acc = jnp.zeros(bk.shape, jnp.float32)
        for b in range(REL_BUCKETS):
            acc = jnp.where(bk == b, tab_ref[h, b], acc)
        out_ref[d] = acc * LOG2E


def _bias_tiles(rel_bias, tq):
    r = jnp.arange(tq, dtype=jnp.int32)
    rel = r[:, None] - r[None, :]
    buckets = jnp.stack([_t5_bucket(rel), _t5_bucket(rel - tq)])
    return pl.pallas_call(
        _bias_tile_kernel,
        grid=(N_HEADS,),
        in_specs=[
            pl.BlockSpec(memory_space=pltpu.SMEM),
            pl.BlockSpec((2, tq, tq), lambda h: (0, 0, 0)),
        ],
        out_specs=pl.BlockSpec((None, 2, tq, tq), lambda h: (h, 0, 0, 0)),
        out_shape=jax.ShapeDtypeStruct((N_HEADS, 2, tq, tq), jnp.float32),
        compiler_params=_cparams(("parallel",)),
        name="t5_bias_tiles",
    )(rel_bias.T.astype(jnp.float32), buckets)


def _dsa_kernel(tab_ref, q_ref, k_ref, v_ref, mask_ref, bias_ref, o_ref, zero_ref, acc_ref, *, tq):
    i = pl.program_id(1)
    h = pl.program_id(2)
    zero_ref[...] = jnp.zeros_like(zero_ref)
    far_bias = tab_ref[h, FAR_BUCKET] * LOG2E
    q_t = q_ref[...].astype(jnp.float32).T.astype(jnp.bfloat16)

    def update(j, m, l, near):
        rows = pl.ds(pl.multiple_of(j * tq, tq), tq)
        s = jnp.dot(k_ref[rows, :], q_t, preferred_element_type=jnp.float32)
        t = s * (FOX_SCALE * LOG2E) + mask_ref[j].astype(jnp.float32)
        if near is None:
            shift = far_bias
        else:
            t = t + bias_ref[near]
            shift = 0.0
        m_new = jnp.maximum(m, jnp.max(t, axis=0, keepdims=True) + shift)
        p = jnp.exp2(t - (m_new - shift))
        alpha = jnp.exp2(m - m_new)
        l_new = alpha * l + jnp.sum(p, axis=0, keepdims=True)
        pv = lax.dot_general(v_ref[rows, :], p.astype(jnp.bfloat16), (((0,), (0,)), ((), ())),
                             preferred_element_type=jnp.float32)
        acc_ref[...] = alpha * acc_ref[...] + pv
        return m_new, l_new

    acc_ref[...] = jnp.zeros_like(acc_ref)
    m0 = jnp.full((1, tq), NEG_BIG, jnp.float32)
    l0 = jnp.zeros((1, tq), jnp.float32)
    carry = lax.fori_loop(0, i - 1, lambda j, c: update(j, c[0], c[1], None), (m0, l0))
    m, l = lax.cond(i >= 1, lambda c: update(i - 1, c[0], c[1], 1), lambda c: c, carry)
    m, l = update(i, m, l, 0)
    o_ref[...] = (acc_ref[...] / l).T.astype(o_ref.dtype)


def _dsa_attention(big3, mask, bias_near, rel_bias, zero_shape, tq=512):
    B, S, _ = big3.shape
    nq = S // tq
    nc = mask.shape[1]
    assert mask.shape[2] == tq and tq >= REL_MAX_DIST and tq % CHUNK == 0
    hb = HEAD_WIDTH // HEAD_DIM
    steps = B * nq * N_HEADS
    zrows = zero_shape[0] // steps
    assert zrows * steps == zero_shape[0] and zrows % 8 == 0
    step = lambda b, i, h: ((b * nq + i) * N_HEADS + h, 0)
    return pl.pallas_call(
        functools.partial(_dsa_kernel, tq=tq),
        grid=(B, nq, N_HEADS),
        in_specs=[
            pl.BlockSpec(memory_space=pltpu.SMEM),
            pl.BlockSpec((None, tq, HEAD_DIM), lambda b, i, h: (b, i, COL_DQ * hb + h)),
            pl.BlockSpec((None, S, HEAD_DIM), lambda b, i, h: (b, 0, COL_DK * hb + h)),
            pl.BlockSpec((None, S, HEAD_DIM), lambda b, i, h: (b, 0, COL_DV * hb + h)),
            pl.BlockSpec((None, nc, tq, tq), lambda b, i, h: (b, 0, 0, i)),
            pl.BlockSpec((None, 2, tq, tq), lambda b, i, h: (h, 0, 0, 0)),
        ],
        out_specs=[
            pl.BlockSpec((None, tq, HEAD_DIM), lambda b, i, h: (b, i, h)),
            pl.BlockSpec((zrows, zero_shape[1]), step),
        ],
        out_shape=[
            jax.ShapeDtypeStruct((B, S, HEAD_WIDTH), jnp.bfloat16),
            jax.ShapeDtypeStruct(zero_shape, jnp.float32),
        ],
        scratch_shapes=[pltpu.VMEM((HEAD_DIM, tq), jnp.float32)],
        compiler_params=_cparams(("parallel", "parallel", "arbitrary")),
        name="dsa_attn",
    )(rel_bias.T.astype(jnp.float32), big3, big3, big3, mask, bias_near)


N_EXPERTS = 32
EXPERT_TOPK = 4
SWIGLU_ALPHA = 1.702
SWIGLU_LIMIT = 7.0


def _outproj_router_kernel(of_ref, od_ref, x_ref, wof_ref, wod_ref, g_ref, rwh_ref, rwl_ref, rb_ref,
                           x1_ref, xn_ref, eidx_ref, rank_ref, gate_ref, cnt_ref, carry_ref, *, tm):
    i = pl.program_id(0)

    @pl.when(i == 0)
    def _():
        carry_ref[...] = jnp.zeros_like(carry_ref)

    x1 = (x_ref[...]
          + jnp.dot(of_ref[...], wof_ref[...], preferred_element_type=jnp.float32)
          + jnp.dot(od_ref[...], wod_ref[...], preferred_element_type=jnp.float32))
    x1_ref[...] = x1
    ms = jnp.mean(x1 * x1, axis=-1, keepdims=True)
    xn = x1 * lax.rsqrt(ms + NORM_EPS) * g_ref[...]
    xn_ref[...] = xn
    xh = xn.astype(jnp.bfloat16)
    xl = (xn - xh.astype(jnp.float32)).astype(jnp.bfloat16)
    nt = (((1,), (1,)), ((), ()))
    logits = (lax.dot_general(rwh_ref[...], xh, nt, preferred_element_type=jnp.float32)
              + lax.dot_general(rwh_ref[...], xl, nt, preferred_element_type=jnp.float32)
              + lax.dot_general(rwl_ref[...], xh, nt, preferred_element_type=jnp.float32)
              + rb_ref[...])
    eio = lax.broadcasted_iota(jnp.int32, (N_EXPERTS, tm), 0)
    work = logits
    vals, idxs = [], []
    multihot = jnp.zeros((N_EXPERTS, tm), jnp.float32)
    for _ in range(EXPERT_TOPK):
        mx = jnp.max(work, axis=0, keepdims=True)
        ix = jnp.min(jnp.where(work == mx, eio, N_EXPERTS), axis=0, keepdims=True)
        hit = eio == ix
        multihot = jnp.where(hit, 1.0, multihot)
        work = jnp.where(hit, -jnp.inf, work)
        vals.append(mx)
        idxs.append(ix)
    ex = [jnp.exp(v - vals[0]) for v in vals]
    den = ex[0] + ex[1] + ex[2] + ex[3]
    a = lax.broadcasted_iota(jnp.int32, (tm, tm), 0)
    b = lax.broadcasted_iota(jnp.int32, (tm, tm), 1)
    upper = jnp.where(a < b, 1.0, 0.0).astype(jnp.bfloat16)
    before = jnp.dot(multihot.astype(jnp.bfloat16), upper,
                     preferred_element_type=jnp.float32) + carry_ref[:, 0:1]
    for r in range(EXPERT_TOPK):
        eidx_ref[r:r + 1, :] = idxs[r]
        gate_ref[r:r + 1, :] = ex[r] / den
        rank_ref[r:r + 1, :] = jnp.sum(jnp.where(eio == idxs[r], before, 0.0),
                                       axis=0, keepdims=True).astype(jnp.int32)
    carry_ref[...] = carry_ref[...] + jnp.sum(multihot, axis=1, keepdims=True)
    cnt_ref[...] = carry_ref[...].astype(jnp.int32)


def _outproj_router(o_fox, o_dsa, x2, wo_f, wo_d, g, rw_hi, rw_lo, rb, tm=512):
    T, D = x2.shape
    hw = o_fox.shape[1]
    full = lambda shape: pl.BlockSpec(shape, lambda i: tuple(0 for _ in shape))
    return pl.pallas_call(
        functools.partial(_outproj_router_kernel, tm=tm),
        grid=(T // tm,),
        in_specs=[
            pl.BlockSpec((tm, hw), lambda i: (i, 0)),
            pl.BlockSpec((tm, hw), lambda i: (i, 0)),
            pl.BlockSpec((tm, D), lambda i: (i, 0)),
            full((hw, D)), full((hw, D)), full((1, D)),
            full((N_EXPERTS, D)), full((N_EXPERTS, D)), full((N_EXPERTS, 1)),
        ],
        out_specs=[
            pl.BlockSpec((tm, D), lambda i: (i, 0)),
            pl.BlockSpec((tm, D), lambda i: (i, 0)),
            pl.BlockSpec((EXPERT_TOPK, tm), lambda i: (0, i)),
            pl.BlockSpec((EXPERT_TOPK, tm), lambda i: (0, i)),
            pl.BlockSpec((EXPERT_TOPK, tm), lambda i: (0, i)),
            full((N_EXPERTS, 128)),
        ],
        out_shape=[
            jax.ShapeDtypeStruct((T, D), jnp.float32),
            jax.ShapeDtypeStruct((T, D), jnp.float32),
            jax.ShapeDtypeStruct((EXPERT_TOPK, T), jnp.int32),
            jax.ShapeDtypeStruct((EXPERT_TOPK, T), jnp.int32),
            jax.ShapeDtypeStruct((EXPERT_TOPK, T), jnp.float32),
            jax.ShapeDtypeStruct((N_EXPERTS, 128), jnp.int32),
        ],
        scratch_shapes=[pltpu.VMEM((N_EXPERTS, 128), jnp.float32)],
        compiler_params=_cparams(("arbitrary",)),
        name="outproj_router",
    )(o_fox, o_dsa, x2, wo_f, wo_d, g, rw_hi, rw_lo, rb)


ROW_UNROLL = 4


def _for_each_row_copy(tt, fn):
    def body(g, c):
        for u in range(ROW_UNROLL):
            for k in range(EXPERT_TOPK):
                fn(g * ROW_UNROLL + u, k)
        return c

    lax.fori_loop(0, tt // ROW_UNROLL, body, 0)


def _dispatch_kernel(dest_ref, xn_ref, xg_in_ref, xg_ref, sem, *, tt):
    del xg_in_ref

    def row_copy(t, k):
        return pltpu.make_async_copy(xn_ref.at[pl.ds(t, 1)], xg_ref.at[pl.ds(dest_ref[k, t], 1)], sem)

    _for_each_row_copy(tt, lambda t, k: row_copy(t, k).start())
    _for_each_row_copy(tt, lambda t, k: row_copy(t, k).wait())


def _dispatch(dest, xn, xg0, tt=256):
    T, D = xn.shape
    n_rows = xg0.shape[0]
    return pl.pallas_call(
        functools.partial(_dispatch_kernel, tt=tt),
        grid=(T // tt,),
        in_specs=[
            pl.BlockSpec((EXPERT_TOPK, tt), lambda i: (0, i), memory_space=pltpu.SMEM),
            pl.BlockSpec((tt, D), lambda i: (i, 0)),
            pl.BlockSpec(memory_space=pl.ANY),
        ],
        out_specs=pl.BlockSpec(memory_space=pl.ANY),
        out_shape=jax.ShapeDtypeStruct((n_rows, D), xn.dtype),
        scratch_shapes=[pltpu.SemaphoreType.DMA(())],
        input_output_aliases={2: 0},
        compiler_params=_cparams(("arbitrary",)),
        name="moe_dispatch",
    )(dest, xn, xg0)


def _expert_kernel(te_ref, nu_ref, x_ref, w1g_ref, w1l_ref, b1g_ref, b1l_ref, w2_ref, b2_ref,
                   y_ref, xb_ref, acc_ref):
    i = pl.program_id(0)
    f = pl.program_id(1)
    nf = pl.num_programs(1)

    @pl.when(i < nu_ref[0])
    def _():
        @pl.when(f == 0)
        def _():
            xb_ref[...] = x_ref[...].astype(xb_ref.dtype)
            acc_ref[...] = jnp.zeros_like(acc_ref)

        xb = xb_ref[...]
        glu = jnp.dot(xb, w1g_ref[...], preferred_element_type=jnp.float32) + b1g_ref[...]
        lin = jnp.dot(xb, w1l_ref[...], preferred_element_type=jnp.float32) + b1l_ref[...]
        glu = jnp.minimum(glu, SWIGLU_LIMIT)
        lin = jnp.clip(lin, -SWIGLU_LIMIT, SWIGLU_LIMIT)
        act = glu * (1.0 / (1.0 + jnp.exp(-SWIGLU_ALPHA * glu))) * (lin + 1.0)
        acc_ref[...] += jnp.dot(act.astype(jnp.bfloat16), w2_ref[...],
                                preferred_element_type=jnp.float32)

        @pl.when(f == nf - 1)
        def _():
            y_ref[...] = acc_ref[...] + b2_ref[...]

    @pl.when((i >= nu_ref[0]) & (f == nf - 1))
    def _():
        y_ref[...] = jnp.zeros_like(y_ref)


def _experts(tile_expert, n_used, xg, w1g, w1l, b1g, b1l, w2b, b2, tme, tf=1024):
    P, D = xg.shape
    F = w2b.shape[1]
    nf = F // tf
    row = lambda i, f, te, nu: (jnp.minimum(i, nu[0] - 1), 0)
    exp = lambda i, te, nu: te[jnp.minimum(i, nu[0] - 1)]
    fblk = lambda i, f, nu: jnp.where(i < nu[0], f, nf - 1)
    grid_spec = pltpu.PrefetchScalarGridSpec(
        num_scalar_prefetch=2,
        grid=(P // tme, nf),
        in_specs=[
            pl.BlockSpec((tme, D), row),
            pl.BlockSpec((None, D, tf), lambda i, f, te, nu: (exp(i, te, nu), 0, fblk(i, f, nu))),
            pl.BlockSpec((None, D, tf), lambda i, f, te, nu: (exp(i, te, nu), 0, fblk(i, f, nu))),
            pl.BlockSpec((None, 1, tf), lambda i, f, te, nu: (exp(i, te, nu), 0, fblk(i, f, nu))),
            pl.BlockSpec((None, 1, tf), lambda i, f, te, nu: (exp(i, te, nu), 0, fblk(i, f, nu))),
            pl.BlockSpec((None, tf, D), lambda i, f, te, nu: (exp(i, te, nu), fblk(i, f, nu), 0)),
            pl.BlockSpec((None, 1, D), lambda i, f, te, nu: (exp(i, te, nu), 0, 0)),
        ],
        out_specs=pl.BlockSpec((tme, D), lambda i, f, te, nu: (i, 0)),
        scratch_shapes=[pltpu.VMEM((tme, D), jnp.bfloat16), pltpu.VMEM((tme, D), jnp.float32)],
    )
    return pl.pallas_call(
        _expert_kernel,
        grid_spec=grid_spec,
        out_shape=jax.ShapeDtypeStruct((P, D), jnp.float32),
        compiler_params=_cparams(("arbitrary", "arbitrary")),
        name="moe_experts",
    )(tile_expert, n_used, xg, w1g, w1l, b1g, b1l, w2b, b2)


def _combine_kernel(dest_ref, x1_ref, gt_ref, y_ref, o_ref, buf_ref, sem, *, tt):
    def row_copy(t, k):
        return pltpu.make_async_copy(y_ref.at[pl.ds(dest_ref[k, t], 1)],
                                     buf_ref.at[k, pl.ds(t, 1)], sem)

    _for_each_row_copy(tt, lambda t, k: row_copy(t, k).start())
    _for_each_row_copy(tt, lambda t, k: row_copy(t, k).wait())
    gt = gt_ref[...]
    out = x1_ref[...]
    for k in range(EXPERT_TOPK):
        out = out + gt[:, k:k + 1] * buf_ref[k]
    o_ref[...] = out


def _combine(dest, x1, gates_t, yrows, tt=128):
    T, D = x1.shape
    return pl.pallas_call(
        functools.partial(_combine_kernel, tt=tt),
        grid=(T // tt,),
        in_specs=[
            pl.BlockSpec((EXPERT_TOPK, tt), lambda i: (0, i), memory_space=pltpu.SMEM),
            pl.BlockSpec((tt, D), lambda i: (i, 0)),
            pl.BlockSpec((tt, 8), lambda i: (i, 0)),
            pl.BlockSpec(memory_space=pl.ANY),
        ],
        out_specs=pl.BlockSpec((tt, D), lambda i: (i, 0)),
        out_shape=jax.ShapeDtypeStruct((T, D), x1.dtype),
        scratch_shapes=[pltpu.VMEM((EXPERT_TOPK, tt, D), jnp.float32), pltpu.SemaphoreType.DMA(())],
        compiler_params=_cparams(("arbitrary",)),
        name="moe_combine",
    )(dest, x1, gates_t, yrows)


def kernel_parts(x, attn_norm_g, w_in, fox_gate_b, fox_q_g, fox_k_g, dsa_q_g, dsa_k_g, idx_k_g,
                 rel_bias, w_out, ffn_norm_g, router_w, router_b, w1, b1, w2, b2):
    B, S, D = x.shape
    T = B * S
    l = 0
    f32 = jnp.float32
    wi = w_in[l]
    o = 0
    cols = {}
    for name, width in (("fq", HEAD_WIDTH), ("fk", HEAD_WIDTH), ("fv", HEAD_WIDTH), ("ff", N_HEADS),
                        ("dq", HEAD_WIDTH), ("dk", HEAD_WIDTH), ("dv", HEAD_WIDTH),
                        ("iq", IDX_HEADS * IDX_DIM), ("ik", IDX_DIM), ("iw", IDX_HEADS)):
        cols[name] = wi[:, o:o + width]
        o += width
    w_big = jnp.concatenate([cols[n] for n in ("fq", "fk", "dq", "dk", "fv", "dv", "iq")],
                            axis=1).astype(jnp.bfloat16)
    pad = jnp.zeros((D, SMALL_W - 2 * IDX_DIM - N_HEADS - IDX_HEADS), f32)
    w_small = jnp.concatenate([cols["ik"], cols["ik"], cols["ff"], cols["iw"], pad],
                              axis=1).astype(jnp.bfloat16)
    head_gains = jnp.stack([fox_q_g[l], fox_k_g[l], dsa_q_g[l], dsa_k_g[l]]).reshape(4, 1, HEAD_DIM)
    gate_b128 = jnp.zeros((1, 128), f32).at[0, :N_HEADS].set(fox_gate_b[l])
    ikg128 = jnp.concatenate([idx_k_g[l], idx_k_g[l]]).reshape(1, 128)

    x2 = x.reshape(T, D)
    big, small = _inproj(x2, attn_norm_g[l].reshape(1, D), w_big, w_small, head_gains)
    big3 = big.reshape(B, S, N_BIG_TILES * HEAD_WIDTH)
    small3 = small.reshape(B, S, SMALL_W)
    ccol, crow, ike, iko = _prep(small3, gate_b128, ikg128)
    tq = 512
    ck5 = crow.reshape(B, N_HEADS, S // tq, 1, tq)
    o_fox, w1g, w1l = _fox_attention(big3, ccol, ck5, w1[l], tq=tq)
    topk = min(IDX_TOPK_MAX, S // 4)
    mask, w2b = _indexer_mask(big3, small3, ike, iko, topk, w2[l])
    bias_near = _bias_tiles(rel_bias, tq)
    tme = 512
    n_rows = T * EXPERT_TOPK + N_EXPERTS * tme
    o_dsa, xg0 = _dsa_attention(big3, mask, bias_near, rel_bias, (n_rows, D), tq=tq)
    bf16 = jnp.bfloat16
    wo = w_out[l].astype(bf16)
    rw = router_w[l].T
    rw_hi = rw.astype(bf16)
    rw_lo = (rw - rw_hi.astype(f32)).astype(bf16)
    x1, xn, eidx, rank, gates, cnt = _outproj_router(
        o_fox.reshape(T, HEAD_WIDTH), o_dsa.reshape(T, HEAD_WIDTH), x2, wo[:HEAD_WIDTH], wo[HEAD_WIDTH:],
        ffn_norm_g[l].reshape(1, D), rw_hi, rw_lo, router_b[l].reshape(N_EXPERTS, 1))
    n_tiles = n_rows // tme
    counts = cnt[:, 0]
    tiles_e = (counts + tme - 1) // tme
    tile_end = jnp.cumsum(tiles_e)
    pad_start = (tile_end - tiles_e) * tme
    e_ids = jnp.arange(N_EXPERTS, dtype=jnp.int32)
    dest = rank + jnp.sum(jnp.where(eidx[None] == e_ids[:, None, None], pad_start[:, None, None], 0),
                          axis=0)
    tile_expert = jnp.minimum(
        jnp.sum(tile_end[None, :] <= jnp.arange(n_tiles, dtype=jnp.int32)[:, None], axis=1),
        N_EXPERTS - 1).astype(jnp.int32)
    n_used = tile_end[-1:].astype(jnp.int32)
    xg = _dispatch(dest, xn, xg0)
    F = w2.shape[2]
    b1g = b1[l][:, 0::2].reshape(N_EXPERTS, 1, F)
    b1l = b1[l][:, 1::2].reshape(N_EXPERTS, 1, F)
    yrows = _experts(tile_expert, n_used, xg, w1g, w1l, b1g, b1l, w2b,
                     b2[l].reshape(N_EXPERTS, 1, D), tme)
    gates_t = jnp.pad(gates.T, ((0, 0), (0, 8 - EXPERT_TOPK)))
    out = _combine(dest, x1, gates_t, yrows)
    return dict(o_fox=o_fox, o_dsa=o_dsa, mask=mask, x1=x1, out=out.reshape(B, S, D))


def kernel(x, attn_norm_g, w_in, fox_gate_b, fox_q_g, fox_k_g, dsa_q_g, dsa_k_g, idx_k_g,
           rel_bias, w_out, ffn_norm_g, router_w, router_b, w1, b1, w2, b2):
    return kernel_parts(x, attn_norm_g, w_in, fox_gate_b, fox_q_g, fox_k_g, dsa_q_g, dsa_k_g, idx_k_g,
                        rel_bias, w_out, ffn_norm_g, router_w, router_b, w1, b1, w2, b2)["out"]
```

```python
import functools
import math

import jax
import jax.numpy as jnp
import numpy as np
from jax import lax
from jax.experimental import pallas as pl
from jax.experimental.pallas import tpu as pltpu

HEAD_DIM = 128
N_HEADS = 8
HEAD_WIDTH = N_HEADS * HEAD_DIM
IDX_HEADS = 16
IDX_DIM = 64
NORM_EPS = 1e-6
FOX_SCALE = HEAD_DIM ** -0.5
NEG_BIG = -1e30
LOG2E = math.log2(math.e)

COL_FQ, COL_FK, COL_DQ, COL_DK, COL_FV, COL_DV, COL_IQ = range(7)
N_BIG_TILES = 7
N_NORM_TILES = 4
SMALL_W = 256

VMEM_LIMIT = 56 * 1024 * 1024


def _cparams(sem):
    return pltpu.CompilerParams(dimension_semantics=sem, vmem_limit_bytes=VMEM_LIMIT)


def _inproj_kernel(x_ref, g_ref, wb_ref, ws_ref, hg_ref, big_ref, small_ref, xn_ref):
    j = pl.program_id(1)

    @pl.when(j == 0)
    def _():
        x = x_ref[...]
        ms = jnp.mean(x * x, axis=-1, keepdims=True)
        xn = (x * lax.rsqrt(ms + NORM_EPS) * g_ref[...]).astype(jnp.bfloat16)
        xn_ref[...] = xn
        small_ref[...] = jnp.dot(xn, ws_ref[...], preferred_element_type=jnp.float32)

    @pl.when(j < N_NORM_TILES)
    def _():
        hg = hg_ref[...]
        for c0 in range(0, HEAD_WIDTH, 2 * HEAD_DIM):
            acc = jnp.dot(xn_ref[...], wb_ref[:, c0:c0 + 2 * HEAD_DIM],
                          preferred_element_type=jnp.float32)
            for c in (0, HEAD_DIM):
                t = acc[:, c:c + HEAD_DIM]
                ms = jnp.mean(t * t, axis=-1, keepdims=True)
                big_ref[:, c0 + c:c0 + c + HEAD_DIM] = (
                    t * lax.rsqrt(ms + NORM_EPS) * hg).astype(big_ref.dtype)

    @pl.when(j >= N_NORM_TILES)
    def _():
        big_ref[...] = jnp.dot(xn_ref[...], wb_ref[...],
                               preferred_element_type=jnp.float32).astype(big_ref.dtype)


def _inproj(x2, g, w_big, w_small, head_gains, tm=1024):
    T, D = x2.shape
    return pl.pallas_call(
        _inproj_kernel,
        grid=(T // tm, N_BIG_TILES),
        in_specs=[
            pl.BlockSpec((tm, D), lambda i, j: (i, 0)),
            pl.BlockSpec((1, D), lambda i, j: (0, 0)),
            pl.BlockSpec((D, HEAD_WIDTH), lambda i, j: (0, j)),
            pl.BlockSpec((D, SMALL_W), lambda i, j: (0, 0)),
            pl.BlockSpec((None, 1, HEAD_DIM), lambda i, j: (jnp.minimum(j, N_NORM_TILES - 1), 0, 0)),
        ],
        out_specs=[
            pl.BlockSpec((tm, HEAD_WIDTH), lambda i, j: (i, j)),
            pl.BlockSpec((tm, SMALL_W), lambda i, j: (i, 0)),
        ],
        out_shape=[
            jax.ShapeDtypeStruct((T, N_BIG_TILES * HEAD_WIDTH), jnp.bfloat16),
            jax.ShapeDtypeStruct((T, SMALL_W), jnp.float32),
        ],
        scratch_shapes=[pltpu.VMEM((tm, D), jnp.bfloat16)],
        compiler_params=_cparams(("parallel", "arbitrary")),
        name="inproj",
    )(x2, g, w_big, w_small, head_gains)


def _prep_kernel(small_ref, gb_ref, ikg_ref, ccol_ref, crow_ref, ike_ref, iko_ref, *, tk):
    S = small_ref.shape[0]
    z = small_ref[:, 128:256] + gb_ref[...]
    ls = jnp.minimum(z, 0.0) - jnp.log(1.0 + jnp.exp(-jnp.abs(z)))
    row = lax.broadcasted_iota(jnp.int32, (S, 128), 0)
    c = ls
    sh = 1
    while sh < S:
        c = c + jnp.where(row >= sh, pltpu.roll(c, sh, axis=0), 0.0)
        sh *= 2
    ccol_ref[...] = c
    crow_ref[...] = c.T[0:N_HEADS, :]
    ik = small_ref[:, 0:128]
    ms = jnp.sum(ik * ik, axis=-1, keepdims=True) * (1.0 / 128.0)
    ikn = ik * lax.rsqrt(ms + NORM_EPS) * ikg_ref[...]
    lane = lax.broadcasted_iota(jnp.int32, (S, 128), 1)
    ike = jnp.where(lane < IDX_DIM, ikn, 0.0)
    iko = jnp.where(lane >= IDX_DIM, ikn, 0.0)
    for cidx in range(S // tk):
        ike_ref[cidx] = ike[cidx * tk:(cidx + 1) * tk, :].astype(ike_ref.dtype)
        iko_ref[cidx] = iko[cidx * tk:(cidx + 1) * tk, :].astype(iko_ref.dtype)


def _prep(small3, gate_b128, ikg128, tk=512):
    B, S, _ = small3.shape
    nk = S // tk
    return pl.pallas_call(
        functools.partial(_prep_kernel, tk=tk),
        grid=(B,),
        in_specs=[
            pl.BlockSpec((None, S, SMALL_W), lambda b: (b, 0, 0)),
            pl.BlockSpec((1, 128), lambda b: (0, 0)),
            pl.BlockSpec((1, 128), lambda b: (0, 0)),
        ],
        out_specs=[
            pl.BlockSpec((None, S, 128), lambda b: (b, 0, 0)),
            pl.BlockSpec((None, N_HEADS, S), lambda b: (b, 0, 0)),
            pl.BlockSpec((None, nk, tk, 128), lambda b: (b, 0, 0, 0)),
            pl.BlockSpec((None, nk, tk, 128), lambda b: (b, 0, 0, 0)),
        ],
        out_shape=[
            jax.ShapeDtypeStruct((B, S, 128), jnp.float32),
            jax.ShapeDtypeStruct((B, N_HEADS, S), jnp.float32),
            jax.ShapeDtypeStruct((B, nk, tk, 128), jnp.bfloat16),
            jax.ShapeDtypeStruct((B, nk, tk, 128), jnp.bfloat16),
        ],
        compiler_params=_cparams(("parallel",)),
        name="prep",
    )(small3, gate_b128, ikg128)


MXU_WIDTH = 256


def _w1_split_slab(w_ref, g_ref, l_ref):
    half = MXU_WIDTH // 2
    r = lax.broadcasted_iota(jnp.int32, (MXU_WIDTH, MXU_WIDTH), 0)
    c = lax.broadcasted_iota(jnp.int32, (MXU_WIDTH, MXU_WIDTH), 1)
    src = jnp.where(c < half, 2 * c, 2 * (c - half) + 1)
    perm = jnp.where(r == src, 1.0, 0.0).astype(jnp.bfloat16)
    for n in range(w_ref.shape[0]):
        for k in range(w_ref.shape[2] // MXU_WIDTH):
            wk = w_ref[n, :, k * MXU_WIDTH:(k + 1) * MXU_WIDTH].astype(jnp.bfloat16)
            out = jnp.dot(wk, perm, preferred_element_type=jnp.float32)
            g_ref[n, :, k * half:(k + 1) * half] = out[:, :half].astype(g_ref.dtype)
            l_ref[n, :, k * half:(k + 1) * half] = out[:, half:].astype(l_ref.dtype)


def _w1_slab_plan(w1_shape, steps):
    E, _, F2 = w1_shape
    cols = E * F2 // steps
    assert cols * steps == E * F2 and cols % MXU_WIDTH == 0
    if cols <= F2:
        assert F2 % cols == 0
        return 1, cols
    assert cols % F2 == 0
    return cols // F2, F2


def _fox_kernel(q_ref, k_ref, v_ref, ccol_ref, cq_ref, w1_ref, o_ref, w1g_ref, w1l_ref,
                acc_ref, ckb_ref, *, tq):
    h = pl.program_id(1)
    i = pl.program_id(2)
    S = k_ref.shape[0]

    @pl.when(i == 0)
    def _():
        lane = lax.broadcasted_iota(jnp.int32, (S, 128), 1)
        col = jnp.sum(jnp.where(lane == h, ccol_ref[...], 0.0), axis=-1, keepdims=True)
        ckb_ref[...] = jnp.broadcast_to(col * LOG2E, (S, 128))

    cq = cq_ref[i] * LOG2E
    q_t = q_ref[...].astype(jnp.float32).T.astype(jnp.bfloat16)

    def update(j, m, l, masked):
        rows = pl.ds(pl.multiple_of(j * tq, tq), tq)
        s = jnp.dot(k_ref[rows, :], q_t, preferred_element_type=jnp.float32)
        t = s * (FOX_SCALE * LOG2E) - jnp.concatenate([ckb_ref[rows, :]] * (tq // 128), axis=1)
        if masked:
            r = lax.broadcasted_iota(jnp.int32, (tq, tq), 0)
            c = lax.broadcasted_iota(jnp.int32, (tq, tq), 1)
            t = jnp.where(r <= c, t, NEG_BIG)
        m_new = jnp.maximum(m, jnp.max(t, axis=0, keepdims=True) + cq)
        p = jnp.exp2(t - (m_new - cq))
        alpha = jnp.exp2(m - m_new)
        l_new = alpha * l + jnp.sum(p, axis=0, keepdims=True)
        pv = lax.dot_general(v_ref[rows, :], p.astype(jnp.bfloat16), (((0,), (0,)), ((), ())),
                             preferred_element_type=jnp.float32)
        acc_ref[...] = alpha * acc_ref[...] + pv
        return m_new, l_new

    acc_ref[...] = jnp.zeros_like(acc_ref)
    m0 = jnp.full((1, tq), NEG_BIG, jnp.float32)
    l0 = jnp.zeros((1, tq), jnp.float32)
    m, l = lax.fori_loop(0, i, lambda j, c: update(j, c[0], c[1], False), (m0, l0))
    m, l = update(i, m, l, True)
    o_ref[...] = (acc_ref[...] / l).T.astype(o_ref.dtype)
    _w1_split_slab(w1_ref, w1g_ref, w1l_ref)


def _fox_attention(big3, ccol, ck5, w1, tq=512):
    B, S, _ = big3.shape
    nq = S // tq
    hb = HEAD_WIDTH // HEAD_DIM
    E, D, F2 = w1.shape
    ne, cw = _w1_slab_plan(w1.shape, B * N_HEADS * nq)
    slabs_per_expert = F2 // cw

    def slab(b, h, i):
        s = (b * N_HEADS + h) * nq + i
        return (s // slabs_per_expert, 0, s % slabs_per_expert) if ne == 1 else (s, 0, 0)

    return pl.pallas_call(
        functools.partial(_fox_kernel, tq=tq),
        grid=(B, N_HEADS, nq),
        in_specs=[
            pl.BlockSpec((None, tq, HEAD_DIM), lambda b, h, i: (b, i, COL_FQ * hb + h)),
            pl.BlockSpec((None, S, HEAD_DIM), lambda b, h, i: (b, 0, COL_FK * hb + h)),
            pl.BlockSpec((None, S, HEAD_DIM), lambda b, h, i: (b, 0, COL_FV * hb + h)),
            pl.BlockSpec((None, S, 128), lambda b, h, i: (b, 0, 0)),
            pl.BlockSpec((None, None, nq, 1, tq), lambda b, h, i: (b, h, 0, 0, 0)),
            pl.BlockSpec((ne, D, cw), slab),
        ],
        out_specs=[
            pl.BlockSpec((None, tq, HEAD_DIM), lambda b, h, i: (b, i, h)),
            pl.BlockSpec((ne, D, cw // 2), slab),
            pl.BlockSpec((ne, D, cw // 2), slab),
        ],
        out_shape=[
            jax.ShapeDtypeStruct((B, S, HEAD_WIDTH), jnp.bfloat16),
            jax.ShapeDtypeStruct((E, D, F2 // 2), jnp.bfloat16),
            jax.ShapeDtypeStruct((E, D, F2 // 2), jnp.bfloat16),
        ],
        scratch_shapes=[pltpu.VMEM((HEAD_DIM, tq), jnp.float32), pltpu.VMEM((S, 128), jnp.float32)],
        compiler_params=_cparams(("parallel", "parallel", "arbitrary")),
        name="fox_attn",
    )(big3, big3, big3, ccol, ck5, w1)


CHUNK = 64
IDX_TOPK_MAX = 256
IDX_SCALE = (IDX_DIM ** -0.5) * (IDX_HEADS ** -0.5)
INT_MIN = -2 ** 31
KEY_NEG_INF = -2139095041


def _indexer_kernel(iq_ref, w_ref, ike_ref, iko_ref, w2_ref, mask_ref, w2b_ref, key_ref, iqt_ref,
                    *, tq, tk, topk):
    w2b_ref[...] = w2_ref[...].astype(w2b_ref.dtype)
    i = pl.program_id(1)
    nc = key_ref.shape[0]
    q0 = i * tq
    nch = (q0 + tq + tk - 1) // tk
    qpos = q0 + lax.broadcasted_iota(jnp.int32, (1, tq), 1)
    limit = (qpos // CHUNK + 1) * CHUNK
    w_t = w_ref[...].T
    wrows = [w_t[N_HEADS + h:N_HEADS + h + 1, :] for h in range(IDX_HEADS)]
    for p in range(IDX_HEADS // 2):
        iqt_ref[p] = iq_ref[:, p * 128:(p + 1) * 128].astype(jnp.float32).T.astype(iqt_ref.dtype)
    key_pos = lax.broadcasted_iota(jnp.int32, (tk, tq), 0)

    def score_chunk(c, carry):
        acc = jnp.zeros((tk, tq), jnp.float32)
        for p in range(IDX_HEADS // 2):
            de = jnp.dot(ike_ref[c], iqt_ref[p], preferred_element_type=jnp.float32)
            do = jnp.dot(iko_ref[c], iqt_ref[p], preferred_element_type=jnp.float32)
            acc = acc + wrows[2 * p] * jnp.maximum(de, 0.0) + wrows[2 * p + 1] * jnp.maximum(do, 0.0)
        sc = jnp.where(key_pos + c * tk < limit, acc * IDX_SCALE, -jnp.inf)
        bits = pltpu.bitcast(sc, jnp.int32)
        key_ref[c] = bits ^ ((bits >> 31) & 0x7FFFFFFF)
        return carry

    lax.fori_loop(0, nch, score_chunk, 0)

    def count_keys(pred):
        def body(c, cnt):
            hit = jnp.where(pred(key_ref[c]), 1, 0)
            return cnt + jnp.sum(hit.reshape(tk // 8, 8, tq), axis=0)

        cnt = lax.fori_loop(0, nch, body, jnp.zeros((8, tq), jnp.int32))
        return jnp.sum(cnt, axis=0, keepdims=True)

    def search(it, carry):
        thr, n_ge = carry
        cand = thr + (jnp.int32(1) << (31 - it))
        total = count_keys(lambda k: k >= cand)
        ok = total >= topk
        return jnp.where(ok, cand, thr), jnp.where(ok, total, n_ge)

    thr, n_ge = lax.fori_loop(0, 32, search, (jnp.full((1, tq), INT_MIN, jnp.int32),
                                              jnp.full((1, tq), nc * tk, jnp.int32)))
    tied = jnp.where(n_ge > topk, jnp.where(thr > KEY_NEG_INF, 1, 0), 0)
    any_tied = jnp.sum(tied) > 0

    @pl.when(jnp.logical_not(any_tied))
    def _():
        def write_chunk(c, carry):
            admissible = jnp.where(key_pos + c * tk < limit, 0.0, NEG_BIG)
            mask_ref[c] = jnp.where(key_ref[c] >= thr, admissible, NEG_BIG).astype(mask_ref.dtype)
            return carry

        lax.fori_loop(0, nch, write_chunk, 0)

    @pl.when(any_tied)
    def _():
        need = (topk - count_keys(lambda k: k > thr)).astype(jnp.float32)
        r = lax.broadcasted_iota(jnp.int32, (tk, tk), 0)
        c_ = lax.broadcasted_iota(jnp.int32, (tk, tk), 1)
        lower = jnp.where(c_ <= r, 1.0, 0.0).astype(jnp.bfloat16)

        def write_chunk(c, seen):
            key = key_ref[c]
            eq = jnp.where(key == thr, 1.0, 0.0)
            rank = jnp.dot(lower, eq.astype(jnp.bfloat16), preferred_element_type=jnp.float32) + seen
            take = jnp.where(key > thr, 1.0, jnp.where(rank <= need, eq, 0.0))
            admissible = jnp.where(key_pos + c * tk < limit, 0.0, NEG_BIG)
            mask_ref[c] = jnp.where(take > 0.0, admissible, NEG_BIG).astype(mask_ref.dtype)
            return seen + jnp.sum(eq, axis=0, keepdims=True)

        lax.fori_loop(0, nch, write_chunk, jnp.zeros((1, tq), jnp.float32))

    def fill_chunk(c, carry):
        mask_ref[c] = jnp.full((tk, tq), NEG_BIG, mask_ref.dtype)
        return carry

    lax.fori_loop(nch, nc, fill_chunk, 0)


def _indexer_mask(big3, small3, ike, iko, topk, w2, tq=256):
    B, S, _ = big3.shape
    nc, tk, _ = ike.shape[1:]
    nq = S // tq
    w2_2d = w2.reshape(-1, w2.shape[-1])
    wrows = w2_2d.shape[0] // (B * nq)
    assert wrows * B * nq == w2_2d.shape[0] and wrows % 16 == 0
    mask, w2b = pl.pallas_call(
        functools.partial(_indexer_kernel, tq=tq, tk=tk, topk=topk),
        grid=(B, nq),
        in_specs=[
            pl.BlockSpec((None, tq, HEAD_WIDTH), lambda b, i: (b, i, COL_IQ)),
            pl.BlockSpec((None, tq, 128), lambda b, i: (b, i, 1)),
            pl.BlockSpec((None, nc, tk, 128), lambda b, i: (b, 0, 0, 0)),
            pl.BlockSpec((None, nc, tk, 128), lambda b, i: (b, 0, 0, 0)),
            pl.BlockSpec((wrows, w2_2d.shape[1]), lambda b, i: (b * nq + i, 0)),
        ],
        out_specs=[
            pl.BlockSpec((None, nc, tk, tq), lambda b, i: (b, 0, 0, i)),
            pl.BlockSpec((wrows, w2_2d.shape[1]), lambda b, i: (b * nq + i, 0)),
        ],
        out_shape=[
            jax.ShapeDtypeStruct((B, nc, tk, S), jnp.bfloat16),
            jax.ShapeDtypeStruct(w2_2d.shape, jnp.bfloat16),
        ],
        scratch_shapes=[pltpu.VMEM((nc, tk, tq), jnp.int32),
                        pltpu.VMEM((IDX_HEADS // 2, 128, tq), jnp.bfloat16)],
        compiler_params=_cparams(("parallel", "arbitrary")),
        name="indexer_mask",
    )(big3, small3, ike, iko, w2_2d)
    return mask, w2b.reshape(w2.shape)


REL_BUCKETS = 32
REL_MAX_DIST = 128
FAR_BUCKET = REL_BUCKETS // 2 - 1


def _t5_bucket(rel):
    half = REL_BUCKETS // 2
    max_exact = half // 2
    ret = jnp.where(rel > 0, half, 0)
    n = jnp.abs(rel)
    nf = jnp.maximum(n, 1).astype(jnp.float32)
    large = max_exact + (jnp.log(nf / max_exact) / math.log(REL_MAX_DIST / max_exact)
                         * (half - max_exact)).astype(jnp.int32)
    large = jnp.minimum(large, half - 1)
    return ret + jnp.where(n < max_exact, n, large)


def _bias_tile_kernel(tab_ref, bucket_ref, out_ref):
    h = pl.program_id(0)
    for d in range(2):
        bk = bucket_ref[d]
        acc = jnp.zeros(bk.shape, jnp.float32)
        for b in range(REL_BUCKETS):
            acc = jnp.where(bk == b, tab_ref[h, b], acc)
        out_ref[d] = acc * LOG2E


def _bias_tiles(rel_bias, tq):
    r = jnp.arange(tq, dtype=jnp.int32)
    rel = r[:, None] - r[None, :]
    buckets = jnp.stack([_t5_bucket(rel), _t5_bucket(rel - tq)])
    return pl.pallas_call(
        _bias_tile_kernel,
        grid=(N_HEADS,),
        in_specs=[
            pl.BlockSpec(memory_space=pltpu.SMEM),
            pl.BlockSpec((2, tq, tq), lambda h: (0, 0, 0)),
        ],
        out_specs=pl.BlockSpec((None, 2, tq, tq), lambda h: (h, 0, 0, 0)),
        out_shape=jax.ShapeDtypeStruct((N_HEADS, 2, tq, tq), jnp.float32),
        compiler_params=_cparams(("parallel",)),
        name="t5_bias_tiles",
    )(rel_bias.T.astype(jnp.float32), buckets)


def _dsa_kernel(tab_ref, q_ref, k_ref, v_ref, mask_ref, bias_ref, o_ref, zero_ref, acc_ref, *, tq):
    i = pl.program_id(1)
    h = pl.program_id(2)
    zero_ref[...] = jnp.zeros_like(zero_ref)
    far_bias = tab_ref[h, FAR_BUCKET] * LOG2E
    q_t = q_ref[...].astype(jnp.float32).T.astype(jnp.bfloat16)

    def update(j, m, l, near):
        rows = pl.ds(pl.multiple_of(j * tq, tq), tq)
        s = jnp.dot(k_ref[rows, :], q_t, preferred_element_type=jnp.float32)
        t = s * (FOX_SCALE * LOG2E) + mask_ref[j].astype(jnp.float32)
        if near is None:
            shift = far_bias
        else:
            t = t + bias_ref[near]
            shift = 0.0
        m_new = jnp.maximum(m, jnp.max(t, axis=0, keepdims=True) + shift)
        p = jnp.exp2(t - (m_new - shift))
        alpha = jnp.exp2(m - m_new)
        l_new = alpha * l + jnp.sum(p, axis=0, keepdims=True)
        pv = lax.dot_general(v_ref[rows, :], p.astype(jnp.bfloat16), (((0,), (0,)), ((), ())),
                             preferred_element_type=jnp.float32)
        acc_ref[...] = alpha * acc_ref[...] + pv
        return m_new, l_new

    acc_ref[...] = jnp.zeros_like(acc_ref)
    m0 = jnp.full((1, tq), NEG_BIG, jnp.float32)
    l0 = jnp.zeros((1, tq), jnp.float32)
    carry = lax.fori_loop(0, i - 1, lambda j, c: update(j, c[0], c[1], None), (m0, l0))
    m, l = lax.cond(i >= 1, lambda c: update(i - 1, c[0], c[1], 1), lambda c: c, carry)
    m, l = update(i, m, l, 0)
    o_ref[...] = (acc_ref[...] / l).T.astype(o_ref.dtype)


def _dsa_attention(big3, mask, bias_near, rel_bias, zero_shape, tq=512):
    B, S, _ = big3.shape
    nq = S // tq
    nc = mask.shape[1]
    assert mask.shape[2] == tq and tq >= REL_MAX_DIST and tq % CHUNK == 0
    hb = HEAD_WIDTH // HEAD_DIM
    steps = B * nq * N_HEADS
    zrows = zero_shape[0] // steps
    assert zrows * steps == zero_shape[0] and zrows % 8 == 0
    step = lambda b, i, h: ((b * nq + i) * N_HEADS + h, 0)
    return pl.pallas_call(
        functools.partial(_dsa_kernel, tq=tq),
        grid=(B, nq, N_HEADS),
        in_specs=[
            pl.BlockSpec(memory_space=pltpu.SMEM),
            pl.BlockSpec((None, tq, HEAD_DIM), lambda b, i, h: (b, i, COL_DQ * hb + h)),
            pl.BlockSpec((None, S, HEAD_DIM), lambda b, i, h: (b, 0, COL_DK * hb + h)),
            pl.BlockSpec((None, S, HEAD_DIM), lambda b, i, h: (b, 0, COL_DV * hb + h)),
            pl.BlockSpec((None, nc, tq, tq), lambda b, i, h: (b, 0, 0, i)),
            pl.BlockSpec((None, 2, tq, tq), lambda b, i, h: (h, 0, 0, 0)),
        ],
        out_specs=[
            pl.BlockSpec((None, tq, HEAD_DIM), lambda b, i, h: (b, i, h)),
            pl.BlockSpec((zrows, zero_shape[1]), step),
        ],
        out_shape=[
            jax.ShapeDtypeStruct((B, S, HEAD_WIDTH), jnp.bfloat16),
            jax.ShapeDtypeStruct(zero_shape, jnp.float32),
        ],
        scratch_shapes=[pltpu.VMEM((HEAD_DIM, tq), jnp.float32)],
        compiler_params=_cparams(("parallel", "parallel", "arbitrary")),
        name="dsa_attn",
    )(rel_bias.T.astype(jnp.float32), big3, big3, big3, mask, bias_near)


N_EXPERTS = 32
EXPERT_TOPK = 4
SWIGLU_ALPHA = 1.702
SWIGLU_LIMIT = 7.0


def _outproj_router_kernel(of_ref, od_ref, x_ref, wof_ref, wod_ref, g_ref, rwh_ref, rwl_ref, rb_ref,
                           x1_ref, xn_ref, eidx_ref, rank_ref, gate_ref, cnt_ref, carry_ref, *, tm):
    i = pl.program_id(0)

    @pl.when(i == 0)
    def _():
        carry_ref[...] = jnp.zeros_like(carry_ref)

    x1 = (x_ref[...]
          + jnp.dot(of_ref[...], wof_ref[...], preferred_element_type=jnp.float32)
          + jnp.dot(od_ref[...], wod_ref[...], preferred_element_type=jnp.float32))
    x1_ref[...] = x1
    ms = jnp.mean(x1 * x1, axis=-1, keepdims=True)
    xn = x1 * lax.rsqrt(ms + NORM_EPS) * g_ref[...]
    xn_ref[...] = xn
    xh = xn.astype(jnp.bfloat16)
    xl = (xn - xh.astype(jnp.float32)).astype(jnp.bfloat16)
    nt = (((1,), (1,)), ((), ()))
    logits = (lax.dot_general(rwh_ref[...], xh, nt, preferred_element_type=jnp.float32)
              + lax.dot_general(rwh_ref[...], xl, nt, preferred_element_type=jnp.float32)
              + lax.dot_general(rwl_ref[...], xh, nt, preferred_element_type=jnp.float32)
              + rb_ref[...])
    eio = lax.broadcasted_iota(jnp.int32, (N_EXPERTS, tm), 0)
    work = logits
    vals, idxs = [], []
    multihot = jnp.zeros((N_EXPERTS, tm), jnp.float32)
    for _ in range(EXPERT_TOPK):
        mx = jnp.max(work, axis=0, keepdims=True)
        ix = jnp.min(jnp.where(work == mx, eio, N_EXPERTS), axis=0, keepdims=True)
        hit = eio == ix
        multihot = jnp.where(hit, 1.0, multihot)
        work = jnp.where(hit, -jnp.inf, work)
        vals.append(mx)
        idxs.append(ix)
    ex = [jnp.exp(v - vals[0]) for v in vals]
    den = ex[0] + ex[1] + ex[2] + ex[3]
    a = lax.broadcasted_iota(jnp.int32, (tm, tm), 0)
    b = lax.broadcasted_iota(jnp.int32, (tm, tm), 1)
    upper = jnp.where(a < b, 1.0, 0.0).astype(jnp.bfloat16)
    before = jnp.dot(multihot.astype(jnp.bfloat16), upper,
                     preferred_element_type=jnp.float32) + carry_ref[:, 0:1]
    for r in range(EXPERT_TOPK):
        eidx_ref[r:r + 1, :] = idxs[r]
        gate_ref[r:r + 1, :] = ex[r] / den
        rank_ref[r:r + 1, :] = jnp.sum(jnp.where(eio == idxs[r], before, 0.0),
                                       axis=0, keepdims=True).astype(jnp.int32)
    carry_ref[...] = carry_ref[...] + jnp.sum(multihot, axis=1, keepdims=True)
    cnt_ref[...] = carry_ref[...].astype(jnp.int32)


def _outproj_router(o_fox, o_dsa, x2, wo_f, wo_d, g, rw_hi, rw_lo, rb, tm=512):
    T, D = x2.shape
    hw = o_fox.shape[1]
    full = lambda shape: pl.BlockSpec(shape, lambda i: tuple(0 for _ in shape))
    return pl.pallas_call(
        functools.partial(_outproj_router_kernel, tm=tm),
        grid=(T // tm,),
        in_specs=[
            pl.BlockSpec((tm, hw), lambda i: (i, 0)),
            pl.BlockSpec((tm, hw), lambda i: (i, 0)),
            pl.BlockSpec((tm, D), lambda i: (i, 0)),
            full((hw, D)), full((hw, D)), full((1, D)),
            full((N_EXPERTS, D)), full((N_EXPERTS, D)), full((N_EXPERTS, 1)),
        ],
        out_specs=[
            pl.BlockSpec((tm, D), lambda i: (i, 0)),
            pl.BlockSpec((tm, D), lambda i: (i, 0)),
            pl.BlockSpec((EXPERT_TOPK, tm), lambda i: (0, i)),
            pl.BlockSpec((EXPERT_TOPK, tm), lambda i: (0, i)),
            pl.BlockSpec((EXPERT_TOPK, tm), lambda i: (0, i)),
            full((N_EXPERTS, 128)),
        ],
        out_shape=[
            jax.ShapeDtypeStruct((T, D), jnp.float32),
            jax.ShapeDtypeStruct((T, D), jnp.float32),
            jax.ShapeDtypeStruct((EXPERT_TOPK, T), jnp.int32),
            jax.ShapeDtypeStruct((EXPERT_TOPK, T), jnp.int32),
            jax.ShapeDtypeStruct((EXPERT_TOPK, T), jnp.float32),
            jax.ShapeDtypeStruct((N_EXPERTS, 128), jnp.int32),
        ],
        scratch_shapes=[pltpu.VMEM((N_EXPERTS, 128), jnp.float32)],
        compiler_params=_cparams(("arbitrary",)),
        name="outproj_router",
    )(o_fox, o_dsa, x2, wo_f, wo_d, g, rw_hi, rw_lo, rb)


ROW_UNROLL = 4


def _for_each_row_copy(tt, fn):
    def body(g, c):
        for u in range(ROW_UNROLL):
            for k in range(EXPERT_TOPK):
                fn(g * ROW_UNROLL + u, k)
        return c

    lax.fori_loop(0, tt // ROW_UNROLL, body, 0)


def _dispatch_kernel(dest_ref, xn_ref, xg_in_ref, xg_ref, sem, *, tt):
    del xg_in_ref

    def row_copy(t, k):
        return pltpu.make_async_copy(xn_ref.at[pl.ds(t, 1)], xg_ref.at[pl.ds(dest_ref[k, t], 1)], sem)

    _for_each_row_copy(tt, lambda t, k: row_copy(t, k).start())
    _for_each_row_copy(tt, lambda t, k: row_copy(t, k).wait())


def _dispatch(dest, xn, xg0, tt=256):
    T, D = xn.shape
    n_rows = xg0.shape[0]
    return pl.pallas_call(
        functools.partial(_dispatch_kernel, tt=tt),
        grid=(T // tt,),
        in_specs=[
            pl.BlockSpec((EXPERT_TOPK, tt), lambda i: (0, i), memory_space=pltpu.SMEM),
            pl.BlockSpec((tt, D), lambda i: (i, 0)),
            pl.BlockSpec(memory_space=pl.ANY),
        ],
        out_specs=pl.BlockSpec(memory_space=pl.ANY),
        out_shape=jax.ShapeDtypeStruct((n_rows, D), xn.dtype),
        scratch_shapes=[pltpu.SemaphoreType.DMA(())],
        input_output_aliases={2: 0},
        compiler_params=_cparams(("arbitrary",)),
        name="moe_dispatch",
    )(dest, xn, xg0)


def _expert_kernel(te_ref, nu_ref, x_ref, w1g_ref, w1l_ref, b1g_ref, b1l_ref, w2_ref, b2_ref,
                   y_ref, xb_ref, acc_ref):
    i = pl.program_id(0)
    f = pl.program_id(1)
    nf = pl.num_programs(1)

    @pl.when(i < nu_ref[0])
    def _():
        @pl.when(f == 0)
        def _():
            xb_ref[...] = x_ref[...].astype(xb_ref.dtype)
            acc_ref[...] = jnp.zeros_like(acc_ref)

        xb = xb_ref[...]
        glu = jnp.dot(xb, w1g_ref[...], preferred_element_type=jnp.float32) + b1g_ref[...]
        lin = jnp.dot(xb, w1l_ref[...], preferred_element_type=jnp.float32) + b1l_ref[...]
        glu = jnp.minimum(glu, SWIGLU_LIMIT)
        lin = jnp.clip(lin, -SWIGLU_LIMIT, SWIGLU_LIMIT)
        act = glu * (1.0 / (1.0 + jnp.exp(-SWIGLU_ALPHA * glu))) * (lin + 1.0)
        acc_ref[...] += jnp.dot(act.astype(jnp.bfloat16), w2_ref[...],
                                preferred_element_type=jnp.float32)

        @pl.when(f == nf - 1)
        def _():
            y_ref[...] = acc_ref[...] + b2_ref[...]

    @pl.when((i >= nu_ref[0]) & (f == nf - 1))
    def _():
        y_ref[...] = jnp.zeros_like(y_ref)


def _experts(tile_expert, n_used, xg, w1g, w1l, b1g, b1l, w2b, b2, tme, tf=1024):
    P, D = xg.shape
    F = w2b.shape[1]
    nf = F // tf
    row = lambda i, f, te, nu: (jnp.minimum(i, nu[0] - 1), 0)
    exp = lambda i, te, nu: te[jnp.minimum(i, nu[0] - 1)]
    fblk = lambda i, f, nu: jnp.where(i < nu[0], f, nf - 1)
    grid_spec = pltpu.PrefetchScalarGridSpec(
        num_scalar_prefetch=2,
        grid=(P // tme, nf),
        in_specs=[
            pl.BlockSpec((tme, D), row),
            pl.BlockSpec((None, D, tf), lambda i, f, te, nu: (exp(i, te, nu), 0, fblk(i, f, nu))),
            pl.BlockSpec((None, D, tf), lambda i, f, te, nu: (exp(i, te, nu), 0, fblk(i, f, nu))),
            pl.BlockSpec((None, 1, tf), lambda i, f, te, nu: (exp(i, te, nu), 0, fblk(i, f, nu))),
            pl.BlockSpec((None, 1, tf), lambda i, f, te, nu: (exp(i, te, nu), 0, fblk(i, f, nu))),
            pl.BlockSpec((None, tf, D), lambda i, f, te, nu: (exp(i, te, nu), fblk(i, f, nu), 0)),
            pl.BlockSpec((None, 1, D), lambda i, f, te, nu: (exp(i, te, nu), 0, 0)),
        ],
        out_specs=pl.BlockSpec((tme, D), lambda i, f, te, nu: (i, 0)),
        scratch_shapes=[pltpu.VMEM((tme, D), jnp.bfloat16), pltpu.VMEM((tme, D), jnp.float32)],
    )
    return pl.pallas_call(
        _expert_kernel,
        grid_spec=grid_spec,
        out_shape=jax.ShapeDtypeStruct((P, D), jnp.float32),
        compiler_params=_cparams(("arbitrary", "arbitrary")),
        name="moe_experts",
    )(tile_expert, n_used, xg, w1g, w1l, b1g, b1l, w2b, b2)


def _combine_kernel(dest_ref, x1_ref, gt_ref, y_ref, o_ref, buf_ref, sem, *, tt):
    def row_copy(t, k):
        return pltpu.make_async_copy(y_ref.at[pl.ds(dest_ref[k, t], 1)],
                                     buf_ref.at[k, pl.ds(t, 1)], sem)

    _for_each_row_copy(tt, lambda t, k: row_copy(t, k).start())
    _for_each_row_copy(tt, lambda t, k: row_copy(t, k).wait())
    gt = gt_ref[...]
    out = x1_ref[...]
    for k in range(EXPERT_TOPK):
        out = out + gt[:, k:k + 1] * buf_ref[k]
    o_ref[...] = out


def _combine(dest, x1, gates_t, yrows, tt=128):
    T, D = x1.shape
    return pl.pallas_call(
        functools.partial(_combine_kernel, tt=tt),
        grid=(T // tt,),
        in_specs=[
            pl.BlockSpec((EXPERT_TOPK, tt), lambda i: (0, i), memory_space=pltpu.SMEM),
            pl.BlockSpec((tt, D), lambda i: (i, 0)),
            pl.BlockSpec((tt, 8), lambda i: (i, 0)),
            pl.BlockSpec(memory_space=pl.ANY),
        ],
        out_specs=pl.BlockSpec((tt, D), lambda i: (i, 0)),
        out_shape=jax.ShapeDtypeStruct((T, D), x1.dtype),
        scratch_shapes=[pltpu.VMEM((EXPERT_TOPK, tt, D), jnp.float32), pltpu.SemaphoreType.DMA(())],
        compiler_params=_cparams(("arbitrary",)),
        name="moe_combine",
    )(dest, x1, gates_t, yrows)


def kernel_parts(x, attn_norm_g, w_in, fox_gate_b, fox_q_g, fox_k_g, dsa_q_g, dsa_k_g, idx_k_g,
                 rel_bias, w_out, ffn_norm_g, router_w, router_b, w1, b1, w2, b2):
    B, S, D = x.shape
    T = B * S
    l = 0
    f32 = jnp.float32
    wi = w_in[l]
    o = 0
    cols = {}
    for name, width in (("fq", HEAD_WIDTH), ("fk", HEAD_WIDTH), ("fv", HEAD_WIDTH), ("ff", N_HEADS),
                        ("dq", HEAD_WIDTH), ("dk", HEAD_WIDTH), ("dv", HEAD_WIDTH),
                        ("iq", IDX_HEADS * IDX_DIM), ("ik", IDX_DIM), ("iw", IDX_HEADS)):
        cols[name] = wi[:, o:o + width]
        o += width
    w_big = jnp.concatenate([cols[n] for n in ("fq", "fk", "dq", "dk", "fv", "dv", "iq")],
                            axis=1).astype(jnp.bfloat16)
    pad = jnp.zeros((D, SMALL_W - 2 * IDX_DIM - N_HEADS - IDX_HEADS), f32)
    w_small = jnp.concatenate([cols["ik"], cols["ik"], cols["ff"], cols["iw"], pad],
                              axis=1).astype(jnp.bfloat16)
    head_gains = jnp.stack([fox_q_g[l], fox_k_g[l], dsa_q_g[l], dsa_k_g[l]]).reshape(4, 1, HEAD_DIM)
    gate_b128 = jnp.zeros((1, 128), f32).at[0, :N_HEADS].set(fox_gate_b[l])
    ikg128 = jnp.concatenate([idx_k_g[l], idx_k_g[l]]).reshape(1, 128)

    x2 = x.reshape(T, D)
    big, small = _inproj(x2, attn_norm_g[l].reshape(1, D), w_big, w_small, head_gains)
    big3 = big.reshape(B, S, N_BIG_TILES * HEAD_WIDTH)
    small3 = small.reshape(B, S, SMALL_W)
    ccol, crow, ike, iko = _prep(small3, gate_b128, ikg128)
    tq = 512
    ck5 = crow.reshape(B, N_HEADS, S // tq, 1, tq)
    o_fox, w1g, w1l = _fox_attention(big3, ccol, ck5, w1[l], tq=tq)
    topk = min(IDX_TOPK_MAX, S // 4)
    mask, w2b = _indexer_mask(big3, small3, ike, iko, topk, w2[l])
    bias_near = _bias_tiles(rel_bias, tq)
    tme = 512
    n_rows = T * EXPERT_TOPK + N_EXPERTS * tme
    o_dsa, xg0 = _dsa_attention(big3, mask, bias_near, rel_bias, (n_rows, D), tq=tq)
    bf16 = jnp.bfloat16
    wo = w_out[l].astype(bf16)
    rw = router_w[l].T
    rw_hi = rw.astype(bf16)
    rw_lo = (rw - rw_hi.astype(f32)).astype(bf16)
    x1, xn, eidx, rank, gates, cnt = _outproj_router(
        o_fox.reshape(T, HEAD_WIDTH), o_dsa.reshape(T, HEAD_WIDTH), x2, wo[:HEAD_WIDTH], wo[HEAD_WIDTH:],
        ffn_norm_g[l].reshape(1, D), rw_hi, rw_lo, router_b[l].reshape(N_EXPERTS, 1))
    n_tiles = n_rows // tme
    counts = cnt[:, 0]
    tiles_e = (counts + tme - 1) // tme
    tile_end = jnp.cumsum(tiles_e)
    pad_start = (tile_end - tiles_e) * tme
    e_ids = jnp.arange(N_EXPERTS, dtype=jnp.int32)
    dest = rank + jnp.sum(jnp.where(eidx[None] == e_ids[:, None, None], pad_start[:, None, None], 0),
                          axis=0)
    tile_expert = jnp.minimum(
        jnp.sum(tile_end[None, :] <= jnp.arange(n_tiles, dtype=jnp.int32)[:, None], axis=1),
        N_EXPERTS - 1).astype(jnp.int32)
    n_used = tile_end[-1:].astype(jnp.int32)
    xg = _dispatch(dest, xn, xg0)
    F = w2.shape[2]
    b1g = b1[l][:, 0::2].reshape(N_EXPERTS, 1, F)
    b1l = b1[l][:, 1::2].reshape(N_EXPERTS, 1, F)
    yrows = _experts(tile_expert, n_used, xg, w1g, w1l, b1g, b1l, w2b,
                     b2[l].reshape(N_EXPERTS, 1, D), tme)
    gates_t = jnp.pad(gates.T, ((0, 0), (0, 8 - EXPERT_TOPK)))
    out = _combine(dest, x1, gates_t, yrows)
    return dict(o_fox=o_fox, o_dsa=o_dsa, mask=mask, x1=x1, out=out.reshape(B, S, D))


def kernel(x, attn_norm_g, w_in, fox_gate_b, fox_q_g, fox_k_g, dsa_q_g, dsa_k_g, idx_k_g,
           rel_bias, w_out, ffn_norm_g, router_w, router_b, w1, b1, w2, b2):
    return kernel_parts(x, attn_norm_g, w_in, fox_gate_b, fox_q_g, fox_k_g, dsa_q_g, dsa_k_g, idx_k_g,
                        rel_bias, w_out, ffn_norm_g, router_w, router_b, w1, b1, w2, b2)["out"]
```

```python
import functools
import math

import jax
import jax.numpy as jnp
import numpy as np
from jax import lax
from jax.experimental import pallas as pl
from jax.experimental.pallas import tpu as pltpu

HEAD_DIM = 128
N_HEADS = 8
HEAD_WIDTH = N_HEADS * HEAD_DIM
IDX_HEADS = 16
IDX_DIM = 64
NORM_EPS = 1e-6
FOX_SCALE = HEAD_DIM ** -0.5
NEG_BIG = -1e30
LOG2E = math.log2(math.e)
SPAN = 8

COL_FQ, COL_FK, COL_DQ, COL_DK, COL_FV, COL_DV, COL_IQ = range(7)
N_BIG_TILES = 7
N_NORM_TILES = 4
SMALL_W = 256

VMEM_LIMIT = 56 * 1024 * 1024


def _cparams(sem):
    return pltpu.CompilerParams(dimension_semantics=sem, vmem_limit_bytes=VMEM_LIMIT)


def _inproj_kernel(x_ref, g_ref, wb_ref, ws_ref, hg_ref, big_ref, small_ref, xn_ref):
    j = pl.program_id(1)

    @pl.when(j == 0)
    def _():
        x = x_ref[...]
        ms = jnp.mean(x * x, axis=-1, keepdims=True)
        xn = (x * lax.rsqrt(ms + NORM_EPS) * g_ref[...]).astype(jnp.bfloat16)
        xn_ref[...] = xn
        small_ref[...] = jnp.dot(xn, ws_ref[...], preferred_element_type=jnp.float32)

    @pl.when(j < N_NORM_TILES)
    def _():
        hg = hg_ref[...]
        for c0 in range(0, HEAD_WIDTH, 2 * HEAD_DIM):
            acc = jnp.dot(xn_ref[...], wb_ref[:, c0:c0 + 2 * HEAD_DIM],
                          preferred_element_type=jnp.float32)
            for c in (0, HEAD_DIM):
                t = acc[:, c:c + HEAD_DIM]
                ms = jnp.mean(t * t, axis=-1, keepdims=True)
                big_ref[:, c0 + c:c0 + c + HEAD_DIM] = (
                    t * lax.rsqrt(ms + NORM_EPS) * hg).astype(big_ref.dtype)

    @pl.when(j >= N_NORM_TILES)
    def _():
        big_ref[...] = jnp.dot(xn_ref[...], wb_ref[...],
                               preferred_element_type=jnp.float32).astype(big_ref.dtype)


def _inproj(x2, g, w_big, w_small, head_gains, tm=1024):
    T, D = x2.shape
    return pl.pallas_call(
        _inproj_kernel,
        grid=(T // tm, N_BIG_TILES),
        in_specs=[
            pl.BlockSpec((tm, D), lambda i, j: (i, 0)),
            pl.BlockSpec((1, D), lambda i, j: (0, 0)),
            pl.BlockSpec((D, HEAD_WIDTH), lambda i, j: (0, j)),
            pl.BlockSpec((D, SMALL_W), lambda i, j: (0, 0)),
            pl.BlockSpec((None, 1, HEAD_DIM), lambda i, j: (jnp.minimum(j, N_NORM_TILES - 1), 0, 0)),
        ],
        out_specs=[
            pl.BlockSpec((tm, HEAD_WIDTH), lambda i, j: (i, j)),
            pl.BlockSpec((tm, SMALL_W), lambda i, j: (i, 0)),
        ],
        out_shape=[
            jax.ShapeDtypeStruct((T, N_BIG_TILES * HEAD_WIDTH), jnp.bfloat16),
            jax.ShapeDtypeStruct((T, SMALL_W), jnp.float32),
        ],
        scratch_shapes=[pltpu.VMEM((tm, D), jnp.bfloat16)],
        compiler_params=_cparams(("parallel", "arbitrary")),
        name="inproj",
    )(x2, g, w_big, w_small, head_gains)


def _prep_kernel(small_ref, gb_ref, ikg_ref, ccol_ref, crow_ref, ike_ref, iko_ref, *, tk):
    S = small_ref.shape[0]
    z = small_ref[:, 128:256] + gb_ref[...]
    ls = jnp.minimum(z, 0.0) - jnp.log(1.0 + jnp.exp(-jnp.abs(z)))
    row = lax.broadcasted_iota(jnp.int32, (S, 128), 0)
    c = ls
    sh = 1
    while sh < S:
        c = c + jnp.where(row >= sh, pltpu.roll(c, sh, axis=0), 0.0)
        sh *= 2
    ccol_ref[...] = c
    crow_ref[...] = c.T[0:N_HEADS, :]
    ik = small_ref[:, 0:128]
    ms = jnp.sum(ik * ik, axis=-1, keepdims=True) * (1.0 / 128.0)
    ikn = ik * lax.rsqrt(ms + NORM_EPS) * ikg_ref[...]
    lane = lax.broadcasted_iota(jnp.int32, (S, 128), 1)
    ike = jnp.where(lane < IDX_DIM, ikn, 0.0)
    iko = jnp.where(lane >= IDX_DIM, ikn, 0.0)
    for cidx in range(S // tk):
        ike_ref[cidx] = ike[cidx * tk:(cidx + 1) * tk, :].astype(ike_ref.dtype)
        iko_ref[cidx] = iko[cidx * tk:(cidx + 1) * tk, :].astype(iko_ref.dtype)


def _prep(small3, gate_b128, ikg128, tk=512):
    B, S, _ = small3.shape
    nk = S // tk
    return pl.pallas_call(
        functools.partial(_prep_kernel, tk=tk),
        grid=(B,),
        in_specs=[
            pl.BlockSpec((None, S, SMALL_W), lambda b: (b, 0, 0)),
            pl.BlockSpec((1, 128), lambda b: (0, 0)),
            pl.BlockSpec((1, 128), lambda b: (0, 0)),
        ],
        out_specs=[
            pl.BlockSpec((None, S, 128), lambda b: (b, 0, 0)),
            pl.BlockSpec((None, N_HEADS, S), lambda b: (b, 0, 0)),
            pl.BlockSpec((None, nk, tk, 128), lambda b: (b, 0, 0, 0)),
            pl.BlockSpec((None, nk, tk, 128), lambda b: (b, 0, 0, 0)),
        ],
        out_shape=[
            jax.ShapeDtypeStruct((B, S, 128), jnp.float32),
            jax.ShapeDtypeStruct((B, N_HEADS, S), jnp.float32),
            jax.ShapeDtypeStruct((B, nk, tk, 128), jnp.bfloat16),
            jax.ShapeDtypeStruct((B, nk, tk, 128), jnp.bfloat16),
        ],
        compiler_params=_cparams(("parallel",)),
        name="prep",
    )(small3, gate_b128, ikg128)


MXU_WIDTH = 256


def _w1_split_slab(w_ref, g_ref, l_ref):
    half = MXU_WIDTH // 2
    r = lax.broadcasted_iota(jnp.int32, (MXU_WIDTH, MXU_WIDTH), 0)
    c = lax.broadcasted_iota(jnp.int32, (MXU_WIDTH, MXU_WIDTH), 1)
    src = jnp.where(c < half, 2 * c, 2 * (c - half) + 1)
    perm = jnp.where(r == src, 1.0, 0.0).astype(jnp.bfloat16)
    for n in range(w_ref.shape[0]):
        for k in range(w_ref.shape[2] // MXU_WIDTH):
            wk = w_ref[n, :, k * MXU_WIDTH:(k + 1) * MXU_WIDTH].astype(jnp.bfloat16)
            out = jnp.dot(wk, perm, preferred_element_type=jnp.float32)
            g_ref[n, :, k * half:(k + 1) * half] = out[:, :half].astype(g_ref.dtype)
            l_ref[n, :, k * half:(k + 1) * half] = out[:, half:].astype(l_ref.dtype)


def _w1_slab_plan(w1_shape, steps):
    E, _, F2 = w1_shape
    cols = E * F2 // steps
    assert cols * steps == E * F2 and cols % MXU_WIDTH == 0
    if cols <= F2:
        assert F2 % cols == 0
        return 1, cols
    assert cols % F2 == 0
    return cols // F2, F2


def _fox_kernel(q_ref, k_ref, v_ref, ccol_ref, cq_ref, w1_ref, o_ref, w1g_ref, w1l_ref,
                acc_ref, ckb_ref, *, tq):
    h = pl.program_id(1)
    i = pl.program_id(2)
    S = k_ref.shape[0]

    @pl.when(i == 0)
    def _():
        lane = lax.broadcasted_iota(jnp.int32, (S, 128), 1)
        col = jnp.sum(jnp.where(lane == h, ccol_ref[...], 0.0), axis=-1, keepdims=True)
        ckb_ref[...] = jnp.broadcast_to(col * LOG2E, (S, 128))

    cq = cq_ref[i] * LOG2E
    q_t = q_ref[...].astype(jnp.float32).T.astype(jnp.bfloat16)

    def update(j, nt, carry, ends_on_diagonal):
        m, l = carry
        kt = nt * tq
        rows = pl.ds(pl.multiple_of(j * tq, tq), kt)
        s = jnp.dot(k_ref[rows, :], q_t, preferred_element_type=jnp.float32)
        t = s * (FOX_SCALE * LOG2E) - jnp.concatenate([ckb_ref[rows, :]] * (tq // 128), axis=1)
        if ends_on_diagonal:
            r = lax.broadcasted_iota(jnp.int32, (kt, tq), 0) - (kt - tq)
            c = lax.broadcasted_iota(jnp.int32, (kt, tq), 1)
            t = jnp.where(r <= c, t, NEG_BIG)
        m_new = jnp.maximum(m, jnp.max(t, axis=0, keepdims=True) + cq)
        p = jnp.exp2(t - (m_new - cq))
        alpha = jnp.exp2(m - m_new)
        l_new = alpha * l + jnp.sum(p, axis=0, keepdims=True)
        pv = lax.dot_general(v_ref[rows, :], p.astype(jnp.bfloat16), (((0,), (0,)), ((), ())),
                             preferred_element_type=jnp.float32)
        acc_ref[...] = alpha * acc_ref[...] + pv
        return m_new, l_new

    acc_ref[...] = jnp.zeros_like(acc_ref)
    init = (jnp.full((1, tq), NEG_BIG, jnp.float32), jnp.zeros((1, tq), jnp.float32))
    carry = lax.fori_loop(0, i // SPAN, lambda jj, c: update(SPAN * jj, SPAN, c, False), init)
    rem = i % SPAN
    m, l = lax.switch(rem, [functools.partial(lambda c, r: update(i - r, r + 1, c, True), r=r)
                            for r in range(SPAN)], carry)
    o_ref[...] = (acc_ref[...] / l).T.astype(o_ref.dtype)
    _w1_split_slab(w1_ref, w1g_ref, w1l_ref)


def _fox_attention(big3, ccol, ck5, w1, tq=512):
    B, S, _ = big3.shape
    nq = S // tq
    hb = HEAD_WIDTH // HEAD_DIM
    E, D, F2 = w1.shape
    ne, cw = _w1_slab_plan(w1.shape, B * N_HEADS * nq)
    slabs_per_expert = F2 // cw

    def slab(b, h, i):
        s = (b * N_HEADS + h) * nq + i
        return (s // slabs_per_expert, 0, s % slabs_per_expert) if ne == 1 else (s, 0, 0)

    return pl.pallas_call(
        functools.partial(_fox_kernel, tq=tq),
        grid=(B, N_HEADS, nq),
        in_specs=[
            pl.BlockSpec((None, tq, HEAD_DIM), lambda b, h, i: (b, i, COL_FQ * hb + h)),
            pl.BlockSpec((None, S, HEAD_DIM), lambda b, h, i: (b, 0, COL_FK * hb + h)),
            pl.BlockSpec((None, S, HEAD_DIM), lambda b, h, i: (b, 0, COL_FV * hb + h)),
            pl.BlockSpec((None, S, 128), lambda b, h, i: (b, 0, 0)),
            pl.BlockSpec((None, None, nq, 1, tq), lambda b, h, i: (b, h, 0, 0, 0)),
            pl.BlockSpec((ne, D, cw), slab),
        ],
        out_specs=[
            pl.BlockSpec((None, tq, HEAD_DIM), lambda b, h, i: (b, i, h)),
            pl.BlockSpec((ne, D, cw // 2), slab),
            pl.BlockSpec((ne, D, cw // 2), slab),
        ],
        out_shape=[
            jax.ShapeDtypeStruct((B, S, HEAD_WIDTH), jnp.bfloat16),
            jax.ShapeDtypeStruct((E, D, F2 // 2), jnp.bfloat16),
            jax.ShapeDtypeStruct((E, D, F2 // 2), jnp.bfloat16),
        ],
        scratch_shapes=[pltpu.VMEM((HEAD_DIM, tq), jnp.float32), pltpu.VMEM((S, 128), jnp.float32)],
        compiler_params=_cparams(("parallel", "parallel", "arbitrary")),
        name="fox_attn",
    )(big3, big3, big3, ccol, ck5, w1)


CHUNK = 64
IDX_TOPK_MAX = 256
IDX_SCALE = (IDX_DIM ** -0.5) * (IDX_HEADS ** -0.5)
INT_MIN = -2 ** 31
KEY_NEG_INF = -2139095041


def _indexer_kernel(iq_ref, w_ref, ike_ref, iko_ref, w2_ref, mask_ref, w2b_ref, key_ref, iqt_ref,
                    *, tq, tk, topk):
    w2b_ref[...] = w2_ref[...].astype(w2b_ref.dtype)
    i = pl.program_id(1)
    nc = key_ref.shape[0]
    q0 = i * tq
    nch = (q0 + tq + tk - 1) // tk
    qpos = q0 + lax.broadcasted_iota(jnp.int32, (1, tq), 1)
    limit = (qpos // CHUNK + 1) * CHUNK
    w_t = w_ref[...].T
    wrows = [w_t[N_HEADS + h:N_HEADS + h + 1, :] for h in range(IDX_HEADS)]
    for p in range(IDX_HEADS // 2):
        iqt_ref[p] = iq_ref[:, p * 128:(p + 1) * 128].astype(jnp.float32).T.astype(iqt_ref.dtype)
    key_pos = lax.broadcasted_iota(jnp.int32, (tk, tq), 0)

    def score_chunk(c, carry):
        acc = jnp.zeros((tk, tq), jnp.float32)
        for p in range(IDX_HEADS // 2):
            de = jnp.dot(ike_ref[c], iqt_ref[p], preferred_element_type=jnp.float32)
            do = jnp.dot(iko_ref[c], iqt_ref[p], preferred_element_type=jnp.float32)
            acc = acc + wrows[2 * p] * jnp.maximum(de, 0.0) + wrows[2 * p + 1] * jnp.maximum(do, 0.0)
        sc = jnp.where(key_pos + c * tk < limit, acc * IDX_SCALE, -jnp.inf)
        bits = pltpu.bitcast(sc, jnp.int32)
        key_ref[c] = bits ^ ((bits >> 31) & 0x7FFFFFFF)
        return carry

    lax.fori_loop(0, nch, score_chunk, 0)

    def count_keys(pred):
        def body(c, cnt):
            hit = jnp.where(pred(key_ref[c]), 1, 0)
            return cnt + jnp.sum(hit.reshape(tk // 8, 8, tq), axis=0)

        cnt = lax.fori_loop(0, nch, body, jnp.zeros((8, tq), jnp.int32))
        return jnp.sum(cnt, axis=0, keepdims=True)

    def search(it, carry):
        thr, n_ge = carry
        cand = thr + (jnp.int32(1) << (31 - it))
        total = count_keys(lambda k: k >= cand)
        ok = total >= topk
        return jnp.where(ok, cand, thr), jnp.where(ok, total, n_ge)

    thr, n_ge = lax.fori_loop(0, 32, search, (jnp.full((1, tq), INT_MIN, jnp.int32),
                                              jnp.full((1, tq), nc * tk, jnp.int32)))
    tied = jnp.where(n_ge > topk, jnp.where(thr > KEY_NEG_INF, 1, 0), 0)
    any_tied = jnp.sum(tied) > 0

    @pl.when(jnp.logical_not(any_tied))
    def _():
        def write_chunk(c, carry):
            admissible = jnp.where(key_pos + c * tk < limit, 0.0, NEG_BIG)
            mask_ref[c] = jnp.where(key_ref[c] >= thr, admissible, NEG_BIG).astype(mask_ref.dtype)
            return carry

        lax.fori_loop(0, nch, write_chunk, 0)

    @pl.when(any_tied)
    def _():
        need = (topk - count_keys(lambda k: k > thr)).astype(jnp.float32)
        r = lax.broadcasted_iota(jnp.int32, (tk, tk), 0)
        c_ = lax.broadcasted_iota(jnp.int32, (tk, tk), 1)
        lower = jnp.where(c_ <= r, 1.0, 0.0).astype(jnp.bfloat16)

        def write_chunk(c, seen):
            key = key_ref[c]
            eq = jnp.where(key == thr, 1.0, 0.0)
            rank = jnp.dot(lower, eq.astype(jnp.bfloat16), preferred_element_type=jnp.float32) + seen
            take = jnp.where(key > thr, 1.0, jnp.where(rank <= need, eq, 0.0))
            admissible = jnp.where(key_pos + c * tk < limit, 0.0, NEG_BIG)
            mask_ref[c] = jnp.where(take > 0.0, admissible, NEG_BIG).astype(mask_ref.dtype)
            return seen + jnp.sum(eq, axis=0, keepdims=True)

        lax.fori_loop(0, nch, write_chunk, jnp.zeros((1, tq), jnp.float32))

    def fill_chunk(c, carry):
        mask_ref[c] = jnp.full((tk, tq), NEG_BIG, mask_ref.dtype)
        return carry

    lax.fori_loop(nch, nc, fill_chunk, 0)


def _indexer_mask(big3, small3, ike, iko, topk, w2, tq=256):
    B, S, _ = big3.shape
    nc, tk, _ = ike.shape[1:]
    nq = S // tq
    w2_2d = w2.reshape(-1, w2.shape[-1])
    wrows = w2_2d.shape[0] // (B * nq)
    assert wrows * B * nq == w2_2d.shape[0] and wrows % 16 == 0
    mask, w2b = pl.pallas_call(
        functools.partial(_indexer_kernel, tq=tq, tk=tk, topk=topk),
        grid=(B, nq),
        in_specs=[
            pl.BlockSpec((None, tq, HEAD_WIDTH), lambda b, i: (b, i, COL_IQ)),
            pl.BlockSpec((None, tq, 128), lambda b, i: (b, i, 1)),
            pl.BlockSpec((None, nc, tk, 128), lambda b, i: (b, 0, 0, 0)),
            pl.BlockSpec((None, nc, tk, 128), lambda b, i: (b, 0, 0, 0)),
            pl.BlockSpec((wrows, w2_2d.shape[1]), lambda b, i: (b * nq + i, 0)),
        ],
        out_specs=[
            pl.BlockSpec((None, nc, tk, tq), lambda b, i: (b, 0, 0, i)),
            pl.BlockSpec((wrows, w2_2d.shape[1]), lambda b, i: (b * nq + i, 0)),
        ],
        out_shape=[
            jax.ShapeDtypeStruct((B, nc, tk, S), jnp.bfloat16),
            jax.ShapeDtypeStruct(w2_2d.shape, jnp.bfloat16),
        ],
        scratch_shapes=[pltpu.VMEM((nc, tk, tq), jnp.int32),
                        pltpu.VMEM((IDX_HEADS // 2, 128, tq), jnp.bfloat16)],
        compiler_params=_cparams(("parallel", "arbitrary")),
        name="indexer_mask",
    )(big3, small3, ike, iko, w2_2d)
    return mask, w2b.reshape(w2.shape)


REL_BUCKETS = 32
REL_MAX_DIST = 128
FAR_BUCKET = REL_BUCKETS // 2 - 1


def _t5_bucket(rel):
    half = REL_BUCKETS // 2
    max_exact = half // 2
    ret = jnp.where(rel > 0, half, 0)
    n = jnp.abs(rel)
    nf = jnp.maximum(n, 1).astype(jnp.float32)
    large = max_exact + (jnp.log(nf / max_exact) / math.log(REL_MAX_DIST / max_exact)
                         * (half - max_exact)).astype(jnp.int32)
    large = jnp.minimum(large, half - 1)
    return ret + jnp.where(n < max_exact, n, large)


def _bias_tile_kernel(tab_ref, bucket_ref, out_ref):
    h = pl.program_id(0)
    for d in range(2):
        bk = bucket_ref[d]
        acc = jnp.zeros(bk.shape, jnp.float32)
        for b in range(REL_BUCKETS):
            acc = jnp.where(bk == b, tab_ref[h, b], acc)
        out_ref[d] = acc * LOG2E


def _bias_tiles(rel_bias, tq):
    r = jnp.arange(tq, dtype=jnp.int32)
    rel = r[:, None] - r[None, :]
    buckets = jnp.stack([_t5_bucket(rel - tq), _t5_bucket(rel)])
    return pl.pallas_call(
        _bias_tile_kernel,
        grid=(N_HEADS,),
        in_specs=[
            pl.BlockSpec(memory_space=pltpu.SMEM),
            pl.BlockSpec((2, tq, tq), lambda h: (0, 0, 0)),
        ],
        out_specs=pl.BlockSpec((None, 2, tq, tq), lambda h: (h, 0, 0, 0)),
        out_shape=jax.ShapeDtypeStruct((N_HEADS, 2, tq, tq), jnp.float32),
        compiler_params=_cparams(("parallel",)),
        name="t5_bias_tiles",
    )(rel_bias.T.astype(jnp.float32), buckets)


def _dsa_kernel(tab_ref, q_ref, k_ref, v_ref, mask_ref, bias_ref, o_ref, zero_ref, acc_ref, *, tq):
    i = pl.program_id(1)
    h = pl.program_id(2)
    zero_ref[...] = jnp.zeros_like(zero_ref)
    far_bias = tab_ref[h, FAR_BUCKET] * LOG2E
    q_t = q_ref[...].astype(jnp.float32).T.astype(jnp.bfloat16)

    def update(j, nt, carry, bias):
        m, l = carry
        kt = nt * tq
        rows = pl.ds(pl.multiple_of(j * tq, tq), kt)
        s = jnp.dot(k_ref[rows, :], q_t, preferred_element_type=jnp.float32)
        t = s * (FOX_SCALE * LOG2E) + mask_ref[pl.ds(j, nt)].reshape(kt, tq).astype(jnp.float32)
        if bias is None:
            shift = far_bias
        elif bias == "diag":
            t = t + bias_ref[1]
            shift = 0.0
        else:
            tiles = [jnp.full(((nt - 2) * tq, tq), far_bias, jnp.float32)] if nt > 2 else []
            t = t + jnp.concatenate(tiles + [bias_ref[...].reshape(2 * tq, tq)], axis=0)
            shift = 0.0
        m_new = jnp.maximum(m, jnp.max(t, axis=0, keepdims=True) + shift)
        p = jnp.exp2(t - (m_new - shift))
        alpha = jnp.exp2(m - m_new)
        l_new = alpha * l + jnp.sum(p, axis=0, keepdims=True)
        pv = lax.dot_general(v_ref[rows, :], p.astype(jnp.bfloat16), (((0,), (0,)), ((), ())),
                             preferred_element_type=jnp.float32)
        acc_ref[...] = alpha * acc_ref[...] + pv
        return m_new, l_new

    acc_ref[...] = jnp.zeros_like(acc_ref)
    init = (jnp.full((1, tq), NEG_BIG, jnp.float32), jnp.zeros((1, tq), jnp.float32))
    n_far = jnp.maximum(i - 1, 0)
    carry = lax.fori_loop(0, n_far // SPAN, lambda jj, c: update(SPAN * jj, SPAN, c, None), init)
    rem = n_far % SPAN
    tails = [lambda c: update(i, 1, c, "diag")] + [
        functools.partial(lambda c, r: update(i - 1 - r, r + 2, c, "tail"), r=r) for r in range(SPAN)]
    m, l = lax.switch(jnp.where(i == 0, 0, 1 + rem), tails, carry)
    o_ref[...] = (acc_ref[...] / l).T.astype(o_ref.dtype)


def _dsa_attention(big3, mask, bias_near, rel_bias, zero_shape, tq=512):
    B, S, _ = big3.shape
    nq = S // tq
    nc = mask.shape[1]
    assert mask.shape[2] == tq and tq >= REL_MAX_DIST and tq % CHUNK == 0
    hb = HEAD_WIDTH // HEAD_DIM
    steps = B * nq * N_HEADS
    zrows = zero_shape[0] // steps
    assert zrows * steps == zero_shape[0] and zrows % 8 == 0
    step = lambda b, i, h: ((b * nq + i) * N_HEADS + h, 0)
    return pl.pallas_call(
        functools.partial(_dsa_kernel, tq=tq),
        grid=(B, nq, N_HEADS),
        in_specs=[
            pl.BlockSpec(memory_space=pltpu.SMEM),
            pl.BlockSpec((None, tq, HEAD_DIM), lambda b, i, h: (b, i, COL_DQ * hb + h)),
            pl.BlockSpec((None, S, HEAD_DIM), lambda b, i, h: (b, 0, COL_DK * hb + h)),
            pl.BlockSpec((None, S, HEAD_DIM), lambda b, i, h: (b, 0, COL_DV * hb + h)),
            pl.BlockSpec((None, nc, tq, tq), lambda b, i, h: (b, 0, 0, i)),
            pl.BlockSpec((None, 2, tq, tq), lambda b, i, h: (h, 0, 0, 0)),
        ],
        out_specs=[
            pl.BlockSpec((None, tq, HEAD_DIM), lambda b, i, h: (b, i, h)),
            pl.BlockSpec((zrows, zero_shape[1]), step),
        ],
        out_shape=[
            jax.ShapeDtypeStruct((B, S, HEAD_WIDTH), jnp.bfloat16),
            jax.ShapeDtypeStruct(zero_shape, jnp.float32),
        ],
        scratch_shapes=[pltpu.VMEM((HEAD_DIM, tq), jnp.float32)],
        compiler_params=_cparams(("parallel", "parallel", "arbitrary")),
        name="dsa_attn",
    )(rel_bias.T.astype(jnp.float32), big3, big3, big3, mask, bias_near)


N_EXPERTS = 32
EXPERT_TOPK = 4
SWIGLU_ALPHA = 1.702
SWIGLU_LIMIT = 7.0


def _outproj_router_kernel(of_ref, od_ref, x_ref, wof_ref, wod_ref, g_ref, rwh_ref, rwl_ref, rb_ref,
                           x1_ref, xn_ref, eidx_ref, rank_ref, gate_ref, cnt_ref, carry_ref, *, tm):
    i = pl.program_id(0)

    @pl.when(i == 0)
    def _():
        carry_ref[...] = jnp.zeros_like(carry_ref)

    x1 = (x_ref[...]
          + jnp.dot(of_ref[...], wof_ref[...], preferred_element_type=jnp.float32)
          + jnp.dot(od_ref[...], wod_ref[...], preferred_element_type=jnp.float32))
    x1_ref[...] = x1
    ms = jnp.mean(x1 * x1, axis=-1, keepdims=True)
    xn = x1 * lax.rsqrt(ms + NORM_EPS) * g_ref[...]
    xn_ref[...] = xn
    xh = xn.astype(jnp.bfloat16)
    xl = (xn - xh.astype(jnp.float32)).astype(jnp.bfloat16)
    nt = (((1,), (1,)), ((), ()))
    logits = (lax.dot_general(rwh_ref[...], xh, nt, preferred_element_type=jnp.float32)
              + lax.dot_general(rwh_ref[...], xl, nt, preferred_element_type=jnp.float32)
              + lax.dot_general(rwl_ref[...], xh, nt, preferred_element_type=jnp.float32)
              + rb_ref[...])
    eio = lax.broadcasted_iota(jnp.int32, (N_EXPERTS, tm), 0)
    work = logits
    vals, idxs = [], []
    multihot = jnp.zeros((N_EXPERTS, tm), jnp.float32)
    for _ in range(EXPERT_TOPK):
        mx = jnp.max(work, axis=0, keepdims=True)
        ix = jnp.min(jnp.where(work == mx, eio, N_EXPERTS), axis=0, keepdims=True)
        hit = eio == ix
        multihot = jnp.where(hit, 1.0, multihot)
        work = jnp.where(hit, -jnp.inf, work)
        vals.append(mx)
        idxs.append(ix)
    ex = [jnp.exp(v - vals[0]) for v in vals]
    den = ex[0] + ex[1] + ex[2] + ex[3]
    a = lax.broadcasted_iota(jnp.int32, (tm, tm), 0)
    b = lax.broadcasted_iota(jnp.int32, (tm, tm), 1)
    upper = jnp.where(a < b, 1.0, 0.0).astype(jnp.bfloat16)
    before = jnp.dot(multihot.astype(jnp.bfloat16), upper,
                     preferred_element_type=jnp.float32) + carry_ref[:, 0:1]
    for r in range(EXPERT_TOPK):
        eidx_ref[r:r + 1, :] = idxs[r]
        gate_ref[r:r + 1, :] = ex[r] / den
        rank_ref[r:r + 1, :] = jnp.sum(jnp.where(eio == idxs[r], before, 0.0),
                                       axis=0, keepdims=True).astype(jnp.int32)
    carry_ref[...] = carry_ref[...] + jnp.sum(multihot, axis=1, keepdims=True)
    cnt_ref[...] = carry_ref[...].astype(jnp.int32)


def _outproj_router(o_fox, o_dsa, x2, wo_f, wo_d, g, rw_hi, rw_lo, rb, tm=512):
    T, D = x2.shape
    hw = o_fox.shape[1]
    full = lambda shape: pl.BlockSpec(shape, lambda i: tuple(0 for _ in shape))
    return pl.pallas_call(
        functools.partial(_outproj_router_kernel, tm=tm),
        grid=(T // tm,),
        in_specs=[
            pl.BlockSpec((tm, hw), lambda i: (i, 0)),
            pl.BlockSpec((tm, hw), lambda i: (i, 0)),
            pl.BlockSpec((tm, D), lambda i: (i, 0)),
            full((hw, D)), full((hw, D)), full((1, D)),
            full((N_EXPERTS, D)), full((N_EXPERTS, D)), full((N_EXPERTS, 1)),
        ],
        out_specs=[
            pl.BlockSpec((tm, D), lambda i: (i, 0)),
            pl.BlockSpec((tm, D), lambda i: (i, 0)),
            pl.BlockSpec((EXPERT_TOPK, tm), lambda i: (0, i)),
            pl.BlockSpec((EXPERT_TOPK, tm), lambda i: (0, i)),
            pl.BlockSpec((EXPERT_TOPK, tm), lambda i: (0, i)),
            full((N_EXPERTS, 128)),
        ],
        out_shape=[
            jax.ShapeDtypeStruct((T, D), jnp.float32),
            jax.ShapeDtypeStruct((T, D), jnp.float32),
            jax.ShapeDtypeStruct((EXPERT_TOPK, T), jnp.int32),
            jax.ShapeDtypeStruct((EXPERT_TOPK, T), jnp.int32),
            jax.ShapeDtypeStruct((EXPERT_TOPK, T), jnp.float32),
            jax.ShapeDtypeStruct((N_EXPERTS, 128), jnp.int32),
        ],
        scratch_shapes=[pltpu.VMEM((N_EXPERTS, 128), jnp.float32)],
        compiler_params=_cparams(("arbitrary",)),
        name="outproj_router",
    )(o_fox, o_dsa, x2, wo_f, wo_d, g, rw_hi, rw_lo, rb)


ROW_UNROLL = 4


def _for_each_row_copy(tt, fn):
    def body(g, c):
        for u in range(ROW_UNROLL):
            for k in range(EXPERT_TOPK):
                fn(g * ROW_UNROLL + u, k)
        return c

    lax.fori_loop(0, tt // ROW_UNROLL, body, 0)


def _dispatch_kernel(dest_ref, xn_ref, xg_in_ref, xg_ref, sem, *, tt):
    del xg_in_ref

    def row_copy(t, k):
        return pltpu.make_async_copy(xn_ref.at[pl.ds(t, 1)], xg_ref.at[pl.ds(dest_ref[k, t], 1)], sem)

    _for_each_row_copy(tt, lambda t, k: row_copy(t, k).start())
    _for_each_row_copy(tt, lambda t, k: row_copy(t, k).wait())


def _dispatch(dest, xn, xg0, tt=256):
    T, D = xn.shape
    n_rows = xg0.shape[0]
    return pl.pallas_call(
        functools.partial(_dispatch_kernel, tt=tt),
        grid=(T // tt,),
        in_specs=[
            pl.BlockSpec((EXPERT_TOPK, tt), lambda i: (0, i), memory_space=pltpu.SMEM),
            pl.BlockSpec((tt, D), lambda i: (i, 0)),
            pl.BlockSpec(memory_space=pl.ANY),
        ],
        out_specs=pl.BlockSpec(memory_space=pl.ANY),
        out_shape=jax.ShapeDtypeStruct((n_rows, D), xn.dtype),
        scratch_shapes=[pltpu.SemaphoreType.DMA(())],
        input_output_aliases={2: 0},
        compiler_params=_cparams(("arbitrary",)),
        name="moe_dispatch",
    )(dest, xn, xg0)


def _expert_kernel(te_ref, nu_ref, x_ref, w1g_ref, w1l_ref, b1g_ref, b1l_ref, w2_ref, b2_ref,
                   y_ref, xb_ref, acc_ref):
    i = pl.program_id(0)
    f = pl.program_id(1)
    nf = pl.num_programs(1)

    @pl.when(i < nu_ref[0])
    def _():
        @pl.when(f == 0)
        def _():
            xb_ref[...] = x_ref[...].astype(xb_ref.dtype)
            acc_ref[...] = jnp.zeros_like(acc_ref)

        xb = xb_ref[...]
        glu = jnp.dot(xb, w1g_ref[...], preferred_element_type=jnp.float32) + b1g_ref[...]
        lin = jnp.dot(xb, w1l_ref[...], preferred_element_type=jnp.float32) + b1l_ref[...]
        glu = jnp.minimum(glu, SWIGLU_LIMIT)
        lin = jnp.clip(lin, -SWIGLU_LIMIT, SWIGLU_LIMIT)
        act = glu * (1.0 / (1.0 + jnp.exp(-SWIGLU_ALPHA * glu))) * (lin + 1.0)
        acc_ref[...] += jnp.dot(act.astype(jnp.bfloat16), w2_ref[...],
                                preferred_element_type=jnp.float32)

        @pl.when(f == nf - 1)
        def _():
            y_ref[...] = acc_ref[...] + b2_ref[...]

    @pl.when((i >= nu_ref[0]) & (f == nf - 1))
    def _():
        y_ref[...] = jnp.zeros_like(y_ref)


def _experts(tile_expert, n_used, xg, w1g, w1l, b1g, b1l, w2b, b2, tme, tf=1024):
    P, D = xg.shape
    F = w2b.shape[1]
    nf = F // tf
    row = lambda i, f, te, nu: (jnp.minimum(i, nu[0] - 1), 0)
    exp = lambda i, te, nu: te[jnp.minimum(i, nu[0] - 1)]
    fblk = lambda i, f, nu: jnp.where(i < nu[0], f, nf - 1)
    grid_spec = pltpu.PrefetchScalarGridSpec(
        num_scalar_prefetch=2,
        grid=(P // tme, nf),
        in_specs=[
            pl.BlockSpec((tme, D), row),
            pl.BlockSpec((None, D, tf), lambda i, f, te, nu: (exp(i, te, nu), 0, fblk(i, f, nu))),
            pl.BlockSpec((None, D, tf), lambda i, f, te, nu: (exp(i, te, nu), 0, fblk(i, f, nu))),
            pl.BlockSpec((None, 1, tf), lambda i, f, te, nu: (exp(i, te, nu), 0, fblk(i, f, nu))),
            pl.BlockSpec((None, 1, tf), lambda i, f, te, nu: (exp(i, te, nu), 0, fblk(i, f, nu))),
            pl.BlockSpec((None, tf, D), lambda i, f, te, nu: (exp(i, te, nu), fblk(i, f, nu), 0)),
            pl.BlockSpec((None, 1, D), lambda i, f, te, nu: (exp(i, te, nu), 0, 0)),
        ],
        out_specs=pl.BlockSpec((tme, D), lambda i, f, te, nu: (i, 0)),
        scratch_shapes=[pltpu.VMEM((tme, D), jnp.bfloat16), pltpu.VMEM((tme, D), jnp.float32)],
    )
    return pl.pallas_call(
        _expert_kernel,
        grid_spec=grid_spec,
        out_shape=jax.ShapeDtypeStruct((P, D), jnp.float32),
        compiler_params=_cparams(("arbitrary", "arbitrary")),
        name="moe_experts",
    )(tile_expert, n_used, xg, w1g, w1l, b1g, b1l, w2b, b2)


def _combine_kernel(dest_ref, x1_ref, gt_ref, y_ref, o_ref, buf_ref, sem, *, tt):
    def row_copy(t, k):
        return pltpu.make_async_copy(y_ref.at[pl.ds(dest_ref[k, t], 1)],
                                     buf_ref.at[k, pl.ds(t, 1)], sem)

    _for_each_row_copy(tt, lambda t, k: row_copy(t, k).start())
    _for_each_row_copy(tt, lambda t, k: row_copy(t, k).wait())
    gt = gt_ref[...]
    out = x1_ref[...]
    for k in range(EXPERT_TOPK):
        out = out + gt[:, k:k + 1] * buf_ref[k]
    o_ref[...] = out


def _combine(dest, x1, gates_t, yrows, tt=128):
    T, D = x1.shape
    return pl.pallas_call(
        functools.partial(_combine_kernel, tt=tt),
        grid=(T // tt,),
        in_specs=[
            pl.BlockSpec((EXPERT_TOPK, tt), lambda i: (0, i), memory_space=pltpu.SMEM),
            pl.BlockSpec((tt, D), lambda i: (i, 0)),
            pl.BlockSpec((tt, 8), lambda i: (i, 0)),
            pl.BlockSpec(memory_space=pl.ANY),
        ],
        out_specs=pl.BlockSpec((tt, D), lambda i: (i, 0)),
        out_shape=jax.ShapeDtypeStruct((T, D), x1.dtype),
        scratch_shapes=[pltpu.VMEM((EXPERT_TOPK, tt, D), jnp.float32), pltpu.SemaphoreType.DMA(())],
        compiler_params=_cparams(("arbitrary",)),
        name="moe_combine",
    )(dest, x1, gates_t, yrows)


def kernel_parts(x, attn_norm_g, w_in, fox_gate_b, fox_q_g, fox_k_g, dsa_q_g, dsa_k_g, idx_k_g,
                 rel_bias, w_out, ffn_norm_g, router_w, router_b, w1, b1, w2, b2):
    B, S, D = x.shape
    T = B * S
    l = 0
    f32 = jnp.float32
    wi = w_in[l]
    o = 0
    cols = {}
    for name, width in (("fq", HEAD_WIDTH), ("fk", HEAD_WIDTH), ("fv", HEAD_WIDTH), ("ff", N_HEADS),
                        ("dq", HEAD_WIDTH), ("dk", HEAD_WIDTH), ("dv", HEAD_WIDTH),
                        ("iq", IDX_HEADS * IDX_DIM), ("ik", IDX_DIM), ("iw", IDX_HEADS)):
        cols[name] = wi[:, o:o + width]
        o += width
    w_big = jnp.concatenate([cols[n] for n in ("fq", "fk", "dq", "dk", "fv", "dv", "iq")],
                            axis=1).astype(jnp.bfloat16)
    pad = jnp.zeros((D, SMALL_W - 2 * IDX_DIM - N_HEADS - IDX_HEADS), f32)
    w_small = jnp.concatenate([cols["ik"], cols["ik"], cols["ff"], cols["iw"], pad],
                              axis=1).astype(jnp.bfloat16)
    head_gains = jnp.stack([fox_q_g[l], fox_k_g[l], dsa_q_g[l], dsa_k_g[l]]).reshape(4, 1, HEAD_DIM)
    gate_b128 = jnp.zeros((1, 128), f32).at[0, :N_HEADS].set(fox_gate_b[l])
    ikg128 = jnp.concatenate([idx_k_g[l], idx_k_g[l]]).reshape(1, 128)

    x2 = x.reshape(T, D)
    big, small = _inproj(x2, attn_norm_g[l].reshape(1, D), w_big, w_small, head_gains)
    big3 = big.reshape(B, S, N_BIG_TILES * HEAD_WIDTH)
    small3 = small.reshape(B, S, SMALL_W)
    ccol, crow, ike, iko = _prep(small3, gate_b128, ikg128)
    tq = 512
    ck5 = crow.reshape(B, N_HEADS, S // tq, 1, tq)
    o_fox, w1g, w1l = _fox_attention(big3, ccol, ck5, w1[l], tq=tq)
    topk = min(IDX_TOPK_MAX, S // 4)
    mask, w2b = _indexer_mask(big3, small3, ike, iko, topk, w2[l])
    bias_near = _bias_tiles(rel_bias, tq)
    tme = 512
    n_rows = T * EXPERT_TOPK + N_EXPERTS * tme
    o_dsa, xg0 = _dsa_attention(big3, mask, bias_near, rel_bias, (n_rows, D), tq=tq)
    bf16 = jnp.bfloat16
    wo = w_out[l].astype(bf16)
    rw = router_w[l].T
    rw_hi = rw.astype(bf16)
    rw_lo = (rw - rw_hi.astype(f32)).astype(bf16)
    x1, xn, eidx, rank, gates, cnt = _outproj_router(
        o_fox.reshape(T, HEAD_WIDTH), o_dsa.reshape(T, HEAD_WIDTH), x2, wo[:HEAD_WIDTH], wo[HEAD_WIDTH:],
        ffn_norm_g[l].reshape(1, D), rw_hi, rw_lo, router_b[l].reshape(N_EXPERTS, 1))
    n_tiles = n_rows // tme
    counts = cnt[:, 0]
    tiles_e = (counts + tme - 1) // tme
    tile_end = jnp.cumsum(tiles_e)
    pad_start = (tile_end - tiles_e) * tme
    e_ids = jnp.arange(N_EXPERTS, dtype=jnp.int32)
    dest = rank + jnp.sum(jnp.where(eidx[None] == e_ids[:, None, None], pad_start[:, None, None], 0),
                          axis=0)
    tile_expert = jnp.minimum(
        jnp.sum(tile_end[None, :] <= jnp.arange(n_tiles, dtype=jnp.int32)[:, None], axis=1),
        N_EXPERTS - 1).astype(jnp.int32)
    n_used = tile_end[-1:].astype(jnp.int32)
    xg = _dispatch(dest, xn, xg0)
    F = w2.shape[2]
    b1g = b1[l][:, 0::2].reshape(N_EXPERTS, 1, F)
    b1l = b1[l][:, 1::2].reshape(N_EXPERTS, 1, F)
    yrows = _experts(tile_expert, n_used, xg, w1g, w1l, b1g, b1l, w2b,
                     b2[l].reshape(N_EXPERTS, 1, D), tme)
    gates_t = jnp.pad(gates.T, ((0, 0), (0, 8 - EXPERT_TOPK)))
    out = _combine(dest, x1, gates_t, yrows)
    return dict(o_fox=o_fox, o_dsa=o_dsa, mask=mask, x1=x1, out=out.reshape(B, S, D))


def kernel(x, attn_norm_g, w_in, fox_gate_b, fox_q_g, fox_k_g, dsa_q_g, dsa_k_g, idx_k_g,
           rel_bias, w_out, ffn_norm_g, router_w, router_b, w1, b1, w2, b2):
    return kernel_parts(x, attn_norm_g, w_in, fox_gate_b, fox_q_g, fox_k_g, dsa_q_g, dsa_k_g, idx_k_g,
                        rel_bias, w_out, ffn_norm_g, router_w, router_b, w1, b1, w2, b2)["out"]
```

```python
import functools
import math

import jax
import jax.numpy as jnp
from jax import lax
from jax.experimental import pallas as pl
from jax.experimental.pallas import tpu as pltpu

HEAD_DIM = 128
N_HEADS = 8
HEAD_WIDTH = N_HEADS * HEAD_DIM
IDX_HEADS = 16
IDX_DIM = 64
NORM_EPS = 1e-6
FOX_SCALE = HEAD_DIM ** -0.5
NEG_BIG = -1e30
LOG2E = math.log2(math.e)
SPAN = 8

COL_FQ, COL_FK, COL_DQ, COL_DK, COL_FV, COL_DV, COL_IQ = range(7)
N_BIG_TILES = 7
N_NORM_TILES = 4
SMALL_W = 256

VMEM_LIMIT = 56 * 1024 * 1024


def _cparams(sem):
    return pltpu.CompilerParams(dimension_semantics=sem, vmem_limit_bytes=VMEM_LIMIT)


def _inproj_kernel(x_ref, g_ref, wb_ref, ws_ref, hg_ref, big_ref, small_ref, xn_ref):
    j = pl.program_id(1)

    @pl.when(j == 0)
    def _():
        x = x_ref[...]
        ms = jnp.mean(x * x, axis=-1, keepdims=True)
        xn = (x * lax.rsqrt(ms + NORM_EPS) * g_ref[...]).astype(jnp.bfloat16)
        xn_ref[...] = xn
        small_ref[...] = jnp.dot(xn, ws_ref[...], preferred_element_type=jnp.float32)

    @pl.when(j < N_NORM_TILES)
    def _():
        hg = hg_ref[...]
        for c0 in range(0, HEAD_WIDTH, 2 * HEAD_DIM):
            acc = jnp.dot(xn_ref[...], wb_ref[:, c0:c0 + 2 * HEAD_DIM],
                          preferred_element_type=jnp.float32)
            for c in (0, HEAD_DIM):
                t = acc[:, c:c + HEAD_DIM]
                ms = jnp.mean(t * t, axis=-1, keepdims=True)
                big_ref[:, c0 + c:c0 + c + HEAD_DIM] = (
                    t * lax.rsqrt(ms + NORM_EPS) * hg).astype(big_ref.dtype)

    @pl.when(j >= N_NORM_TILES)
    def _():
        big_ref[...] = jnp.dot(xn_ref[...], wb_ref[...],
                               preferred_element_type=jnp.float32).astype(big_ref.dtype)


def _inproj(x2, g, w_big, w_small, head_gains, tm=1024):
    T, D = x2.shape
    return pl.pallas_call(
        _inproj_kernel,
        grid=(T // tm, N_BIG_TILES),
        in_specs=[
            pl.BlockSpec((tm, D), lambda i, j: (i, 0)),
            pl.BlockSpec((1, D), lambda i, j: (0, 0)),
            pl.BlockSpec((D, HEAD_WIDTH), lambda i, j: (0, j)),
            pl.BlockSpec((D, SMALL_W), lambda i, j: (0, 0)),
            pl.BlockSpec((None, 1, HEAD_DIM), lambda i, j: (jnp.minimum(j, N_NORM_TILES - 1), 0, 0)),
        ],
        out_specs=[
            pl.BlockSpec((tm, HEAD_WIDTH), lambda i, j: (i, j)),
            pl.BlockSpec((tm, SMALL_W), lambda i, j: (i, 0)),
        ],
        out_shape=[
            jax.ShapeDtypeStruct((T, N_BIG_TILES * HEAD_WIDTH), jnp.bfloat16),
            jax.ShapeDtypeStruct((T, SMALL_W), jnp.float32),
        ],
        scratch_shapes=[pltpu.VMEM((tm, D), jnp.bfloat16)],
        compiler_params=_cparams(("parallel", "arbitrary")),
        name="inproj",
    )(x2, g, w_big, w_small, head_gains)


def _prep_kernel(small_ref, gb_ref, ikg_ref, ccol_ref, crow_ref, ike_ref, iko_ref, *, tk):
    S = small_ref.shape[0]
    z = small_ref[:, 128:256] + gb_ref[...]
    ls = jnp.minimum(z, 0.0) - jnp.log(1.0 + jnp.exp(-jnp.abs(z)))
    row = lax.broadcasted_iota(jnp.int32, (S, 128), 0)
    c = ls
    sh = 1
    while sh < S:
        c = c + jnp.where(row >= sh, pltpu.roll(c, sh, axis=0), 0.0)
        sh *= 2
    ccol_ref[...] = c
    crow_ref[...] = c.T[0:N_HEADS, :]
    ik = small_ref[:, 0:128]
    ms = jnp.sum(ik * ik, axis=-1, keepdims=True) * (1.0 / 128.0)
    ikn = ik * lax.rsqrt(ms + NORM_EPS) * ikg_ref[...]
    lane = lax.broadcasted_iota(jnp.int32, (S, 128), 1)
    ike = jnp.where(lane < IDX_DIM, ikn, 0.0)
    iko = jnp.where(lane >= IDX_DIM, ikn, 0.0)
    for cidx in range(S // tk):
        ike_ref[cidx] = ike[cidx * tk:(cidx + 1) * tk, :].astype(ike_ref.dtype)
        iko_ref[cidx] = iko[cidx * tk:(cidx + 1) * tk, :].astype(iko_ref.dtype)


def _prep(small3, gate_b128, ikg128, tk=512):
    B, S, _ = small3.shape
    nk = S // tk
    return pl.pallas_call(
        functools.partial(_prep_kernel, tk=tk),
        grid=(B,),
        in_specs=[
            pl.BlockSpec((None, S, SMALL_W), lambda b: (b, 0, 0)),
            pl.BlockSpec((1, 128), lambda b: (0, 0)),
            pl.BlockSpec((1, 128), lambda b: (0, 0)),
        ],
        out_specs=[
            pl.BlockSpec((None, S, 128), lambda b: (b, 0, 0)),
            pl.BlockSpec((None, N_HEADS, S), lambda b: (b, 0, 0)),
            pl.BlockSpec((None, nk, tk, 128), lambda b: (b, 0, 0, 0)),
            pl.BlockSpec((None, nk, tk, 128), lambda b: (b, 0, 0, 0)),
        ],
        out_shape=[
            jax.ShapeDtypeStruct((B, S, 128), jnp.float32),
            jax.ShapeDtypeStruct((B, N_HEADS, S), jnp.float32),
            jax.ShapeDtypeStruct((B, nk, tk, 128), jnp.bfloat16),
            jax.ShapeDtypeStruct((B, nk, tk, 128), jnp.bfloat16),
        ],
        compiler_params=_cparams(("parallel",)),
        name="prep",
    )(small3, gate_b128, ikg128)


MXU_WIDTH = 256


def _w1_split_slab(w_ref, g_ref, l_ref):
    half = MXU_WIDTH // 2
    r = lax.broadcasted_iota(jnp.int32, (MXU_WIDTH, MXU_WIDTH), 0)
    c = lax.broadcasted_iota(jnp.int32, (MXU_WIDTH, MXU_WIDTH), 1)
    src = jnp.where(c < half, 2 * c, 2 * (c - half) + 1)
    perm = jnp.where(r == src, 1.0, 0.0).astype(jnp.bfloat16)
    for n in range(w_ref.shape[0]):
        for k in range(w_ref.shape[2] // MXU_WIDTH):
            wk = w_ref[n, :, k * MXU_WIDTH:(k + 1) * MXU_WIDTH].astype(jnp.bfloat16)
            out = jnp.dot(wk, perm, preferred_element_type=jnp.float32)
            g_ref[n, :, k * half:(k + 1) * half] = out[:, :half].astype(g_ref.dtype)
            l_ref[n, :, k * half:(k + 1) * half] = out[:, half:].astype(l_ref.dtype)


def _w1_slab_plan(w1_shape, steps):
    E, _, F2 = w1_shape
    cols = E * F2 // steps
    assert cols * steps == E * F2 and cols % MXU_WIDTH == 0
    if cols <= F2:
        assert F2 % cols == 0
        return 1, cols
    assert cols % F2 == 0
    return cols // F2, F2


def _fox_kernel(q_ref, k_ref, v_ref, ccol_ref, cq_ref, w1_ref, o_ref, w1g_ref, w1l_ref,
                acc_ref, ckb_ref, *, tq):
    h = pl.program_id(1)
    i = pl.program_id(2)
    S = k_ref.shape[0]

    @pl.when(i == 0)
    def _():
        lane = lax.broadcasted_iota(jnp.int32, (S, 128), 1)
        col = jnp.sum(jnp.where(lane == h, ccol_ref[...], 0.0), axis=-1, keepdims=True)
        ckb_ref[...] = jnp.broadcast_to(col * LOG2E, (S, 128))

    cq = cq_ref[i] * LOG2E
    q_t = q_ref[...].astype(jnp.float32).T.astype(jnp.bfloat16)

    def update(j, nt, carry, ends_on_diagonal):
        m, l = carry
        kt = nt * tq
        rows = pl.ds(pl.multiple_of(j * tq, tq), kt)
        s = jnp.dot(k_ref[rows, :], q_t, preferred_element_type=jnp.float32)
        t = s * (FOX_SCALE * LOG2E) - jnp.concatenate([ckb_ref[rows, :]] * (tq // 128), axis=1)
        if ends_on_diagonal:
            r = lax.broadcasted_iota(jnp.int32, (kt, tq), 0) - (kt - tq)
            c = lax.broadcasted_iota(jnp.int32, (kt, tq), 1)
            t = jnp.where(r <= c, t, NEG_BIG)
        m_new = jnp.maximum(m, jnp.max(t, axis=0, keepdims=True) + cq)
        p = jnp.exp2(t - (m_new - cq))
        alpha = jnp.exp2(m - m_new)
        l_new = alpha * l + jnp.sum(p, axis=0, keepdims=True)
        pv = lax.dot_general(v_ref[rows, :], p.astype(jnp.bfloat16), (((0,), (0,)), ((), ())),
                             preferred_element_type=jnp.float32)
        acc_ref[...] = alpha * acc_ref[...] + pv
        return m_new, l_new

    acc_ref[...] = jnp.zeros_like(acc_ref)
    init = (jnp.full((1, tq), NEG_BIG, jnp.float32), jnp.zeros((1, tq), jnp.float32))
    carry = lax.fori_loop(0, i // SPAN, lambda jj, c: update(SPAN * jj, SPAN, c, False), init)
    rem = i % SPAN
    m, l = lax.switch(rem, [functools.partial(lambda c, r: update(i - r, r + 1, c, True), r=r)
                            for r in range(SPAN)], carry)
    o_ref[...] = (acc_ref[...] / l).T.astype(o_ref.dtype)
    _w1_split_slab(w1_ref, w1g_ref, w1l_ref)


def _fox_attention(big3, ccol, ck5, w1, tq=512):
    B, S, _ = big3.shape
    nq = S // tq
    hb = HEAD_WIDTH // HEAD_DIM
    E, D, F2 = w1.shape
    ne, cw = _w1_slab_plan(w1.shape, B * N_HEADS * nq)
    slabs_per_expert = F2 // cw

    def slab(b, h, i):
        s = (b * N_HEADS + h) * nq + i
        return (s // slabs_per_expert, 0, s % slabs_per_expert) if ne == 1 else (s, 0, 0)

    return pl.pallas_call(
        functools.partial(_fox_kernel, tq=tq),
        grid=(B, N_HEADS, nq),
        in_specs=[
            pl.BlockSpec((None, tq, HEAD_DIM), lambda b, h, i: (b, i, COL_FQ * hb + h)),
            pl.BlockSpec((None, S, HEAD_DIM), lambda b, h, i: (b, 0, COL_FK * hb + h)),
            pl.BlockSpec((None, S, HEAD_DIM), lambda b, h, i: (b, 0, COL_FV * hb + h)),
            pl.BlockSpec((None, S, 128), lambda b, h, i: (b, 0, 0)),
            pl.BlockSpec((None, None, nq, 1, tq), lambda b, h, i: (b, h, 0, 0, 0)),
            pl.BlockSpec((ne, D, cw), slab),
        ],
        out_specs=[
            pl.BlockSpec((None, tq, HEAD_DIM), lambda b, h, i: (b, i, h)),
            pl.BlockSpec((ne, D, cw // 2), slab),
            pl.BlockSpec((ne, D, cw // 2), slab),
        ],
        out_shape=[
            jax.ShapeDtypeStruct((B, S, HEAD_WIDTH), jnp.bfloat16),
            jax.ShapeDtypeStruct((E, D, F2 // 2), jnp.bfloat16),
            jax.ShapeDtypeStruct((E, D, F2 // 2), jnp.bfloat16),
        ],
        scratch_shapes=[pltpu.VMEM((HEAD_DIM, tq), jnp.float32), pltpu.VMEM((S, 128), jnp.float32)],
        compiler_params=_cparams(("parallel", "parallel", "arbitrary")),
        name="fox_attn",
    )(big3, big3, big3, ccol, ck5, w1)


CHUNK = 64
IDX_TOPK_MAX = 256
IDX_SCALE = (IDX_DIM ** -0.5) * (IDX_HEADS ** -0.5)
INT_MIN = -2 ** 31
KEY_NEG_INF = -2139095041


def _indexer_kernel(iq_ref, w_ref, ike_ref, iko_ref, w2_ref, mask_ref, w2b_ref, key_ref, iqt_ref,
                    *, tq, tk, topk):
    w2b_ref[...] = w2_ref[...].astype(w2b_ref.dtype)
    i = pl.program_id(1)
    nc = key_ref.shape[0]
    q0 = i * tq
    nch = (q0 + tq + tk - 1) // tk
    qpos = q0 + lax.broadcasted_iota(jnp.int32, (1, tq), 1)
    limit = (qpos // CHUNK + 1) * CHUNK
    w_t = w_ref[...].T
    wrows = [w_t[N_HEADS + h:N_HEADS + h + 1, :] for h in range(IDX_HEADS)]
    for p in range(IDX_HEADS // 2):
        iqt_ref[p] = iq_ref[:, p * 128:(p + 1) * 128].astype(jnp.float32).T.astype(iqt_ref.dtype)
    key_pos = lax.broadcasted_iota(jnp.int32, (tk, tq), 0)

    def score_chunk(c, carry):
        acc = jnp.zeros((tk, tq), jnp.float32)
        for p in range(IDX_HEADS // 2):
            de = jnp.dot(ike_ref[c], iqt_ref[p], preferred_element_type=jnp.float32)
            do = jnp.dot(iko_ref[c], iqt_ref[p], preferred_element_type=jnp.float32)
            acc = acc + wrows[2 * p] * jnp.maximum(de, 0.0) + wrows[2 * p + 1] * jnp.maximum(do, 0.0)
        sc = jnp.where(key_pos + c * tk < limit, acc * IDX_SCALE, -jnp.inf)
        bits = pltpu.bitcast(sc, jnp.int32)
        key_ref[c] = bits ^ ((bits >> 31) & 0x7FFFFFFF)
        return carry

    lax.fori_loop(0, nch, score_chunk, 0)

    def count_keys(pred):
        def body(c, cnt):
            hit = jnp.where(pred(key_ref[c]), 1, 0)
            return cnt + jnp.sum(hit.reshape(tk // 8, 8, tq), axis=0)

        cnt = lax.fori_loop(0, nch, body, jnp.zeros((8, tq), jnp.int32))
        return jnp.sum(cnt, axis=0, keepdims=True)

    def search(it, carry):
        thr, n_ge = carry
        cand = thr + (jnp.int32(1) << (31 - it))
        total = count_keys(lambda k: k >= cand)
        ok = total >= topk
        return jnp.where(ok, cand, thr), jnp.where(ok, total, n_ge)

    thr, n_ge = lax.fori_loop(0, 32, search, (jnp.full((1, tq), INT_MIN, jnp.int32),
                                              jnp.full((1, tq), nc * tk, jnp.int32)))
    tied = jnp.where(n_ge > topk, jnp.where(thr > KEY_NEG_INF, 1, 0), 0)
    any_tied = jnp.sum(tied) > 0

    @pl.when(jnp.logical_not(any_tied))
    def _():
        def write_chunk(c, carry):
            admissible = jnp.where(key_pos + c * tk < limit, 0.0, NEG_BIG)
            mask_ref[c] = jnp.where(key_ref[c] >= thr, admissible, NEG_BIG).astype(mask_ref.dtype)
            return carry

        lax.fori_loop(0, nch, write_chunk, 0)

    @pl.when(any_tied)
    def _():
        need = (topk - count_keys(lambda k: k > thr)).astype(jnp.float32)
        r = lax.broadcasted_iota(jnp.int32, (tk, tk), 0)
        c_ = lax.broadcasted_iota(jnp.int32, (tk, tk), 1)
        lower = jnp.where(c_ <= r, 1.0, 0.0).astype(jnp.bfloat16)

        def write_chunk(c, seen):
            key = key_ref[c]
            eq = jnp.where(key == thr, 1.0, 0.0)
            rank = jnp.dot(lower, eq.astype(jnp.bfloat16), preferred_element_type=jnp.float32) + seen
            take = jnp.where(key > thr, 1.0, jnp.where(rank <= need, eq, 0.0))
            admissible = jnp.where(key_pos + c * tk < limit, 0.0, NEG_BIG)
            mask_ref[c] = jnp.where(take > 0.0, admissible, NEG_BIG).astype(mask_ref.dtype)
            return seen + jnp.sum(eq, axis=0, keepdims=True)

        lax.fori_loop(0, nch, write_chunk, jnp.zeros((1, tq), jnp.float32))

    def fill_chunk(c, carry):
        mask_ref[c] = jnp.full((tk, tq), NEG_BIG, mask_ref.dtype)
        return carry

    lax.fori_loop(nch, nc, fill_chunk, 0)


def _indexer_mask(big3, small3, ike, iko, topk, w2, tq=256):
    B, S, _ = big3.shape
    nc, tk, _ = ike.shape[1:]
    nq = S // tq
    w2_2d = w2.reshape(-1, w2.shape[-1])
    wrows = w2_2d.shape[0] // (B * nq)
    assert wrows * B * nq == w2_2d.shape[0] and wrows % 16 == 0
    mask, w2b = pl.pallas_call(
        functools.partial(_indexer_kernel, tq=tq, tk=tk, topk=topk),
        grid=(B, nq),
        in_specs=[
            pl.BlockSpec((None, tq, HEAD_WIDTH), lambda b, i: (b, i, COL_IQ)),
            pl.BlockSpec((None, tq, 128), lambda b, i: (b, i, 1)),
            pl.BlockSpec((None, nc, tk, 128), lambda b, i: (b, 0, 0, 0)),
            pl.BlockSpec((None, nc, tk, 128), lambda b, i: (b, 0, 0, 0)),
            pl.BlockSpec((wrows, w2_2d.shape[1]), lambda b, i: (b * nq + i, 0)),
        ],
        out_specs=[
            pl.BlockSpec((None, nc, tk, tq), lambda b, i: (b, 0, 0, i)),
            pl.BlockSpec((wrows, w2_2d.shape[1]), lambda b, i: (b * nq + i, 0)),
        ],
        out_shape=[
            jax.ShapeDtypeStruct((B, nc, tk, S), jnp.bfloat16),
            jax.ShapeDtypeStruct(w2_2d.shape, jnp.bfloat16),
        ],
        scratch_shapes=[pltpu.VMEM((nc, tk, tq), jnp.int32),
                        pltpu.VMEM((IDX_HEADS // 2, 128, tq), jnp.bfloat16)],
        compiler_params=_cparams(("parallel", "arbitrary")),
        name="indexer_mask",
    )(big3, small3, ike, iko, w2_2d)
    return mask, w2b.reshape(w2.shape)


REL_BUCKETS = 32
REL_MAX_DIST = 128
FAR_BUCKET = REL_BUCKETS // 2 - 1


def _t5_bucket(rel):
    half = REL_BUCKETS // 2
    max_exact = half // 2
    ret = jnp.where(rel > 0, half, 0)
    n = jnp.abs(rel)
    nf = jnp.maximum(n, 1).astype(jnp.float32)
    large = max_exact + (jnp.log(nf / max_exact) / math.log(REL_MAX_DIST / max_exact)
                         * (half - max_exact)).astype(jnp.int32)
    large = jnp.minimum(large, half - 1)
    return ret + jnp.where(n < max_exact, n, large)


def _bias_tile_kernel(tab_ref, bucket_ref, out_ref):
    h = pl.program_id(0)
    for d in range(2):
        bk = bucket_ref[d]
        acc = jnp.zeros(bk.shape, jnp.float32)
        for b in range(REL_BUCKETS):
            acc = jnp.where(bk == b, tab_ref[h, b], acc)
        out_ref[d] = acc * LOG2E


def _bias_tiles(rel_bias, tq):
    r = jnp.arange(tq, dtype=jnp.int32)
    rel = r[:, None] - r[None, :]
    buckets = jnp.stack([_t5_bucket(rel - tq), _t5_bucket(rel)])
    return pl.pallas_call(
        _bias_tile_kernel,
        grid=(N_HEADS,),
        in_specs=[
            pl.BlockSpec(memory_space=pltpu.SMEM),
            pl.BlockSpec((2, tq, tq), lambda h: (0, 0, 0)),
        ],
        out_specs=pl.BlockSpec((None, 2, tq, tq), lambda h: (h, 0, 0, 0)),
        out_shape=jax.ShapeDtypeStruct((N_HEADS, 2, tq, tq), jnp.float32),
        compiler_params=_cparams(("parallel",)),
        name="t5_bias_tiles",
    )(rel_bias.T.astype(jnp.float32), buckets)


def _dsa_kernel(tab_ref, q_ref, k_ref, v_ref, mask_ref, bias_ref, o_ref, zero_ref, acc_ref, *, tq):
    i = pl.program_id(1)
    h = pl.program_id(2)
    zero_ref[...] = jnp.zeros_like(zero_ref)
    far_bias = tab_ref[h, FAR_BUCKET] * LOG2E
    q_t = q_ref[...].astype(jnp.float32).T.astype(jnp.bfloat16)

    def update(j, nt, carry, bias):
        m, l = carry
        kt = nt * tq
        rows = pl.ds(pl.multiple_of(j * tq, tq), kt)
        s = jnp.dot(k_ref[rows, :], q_t, preferred_element_type=jnp.float32)
        t = s * (FOX_SCALE * LOG2E) + mask_ref[pl.ds(j, nt)].reshape(kt, tq).astype(jnp.float32)
        if bias is None:
            shift = far_bias
        elif bias == "diag":
            t = t + bias_ref[1]
            shift = 0.0
        else:
            tiles = [jnp.full(((nt - 2) * tq, tq), far_bias, jnp.float32)] if nt > 2 else []
            t = t + jnp.concatenate(tiles + [bias_ref[...].reshape(2 * tq, tq)], axis=0)
            shift = 0.0
        m_new = jnp.maximum(m, jnp.max(t, axis=0, keepdims=True) + shift)
        p = jnp.exp2(t - (m_new - shift))
        alpha = jnp.exp2(m - m_new)
        l_new = alpha * l + jnp.sum(p, axis=0, keepdims=True)
        pv = lax.dot_general(v_ref[rows, :], p.astype(jnp.bfloat16), (((0,), (0,)), ((), ())),
                             preferred_element_type=jnp.float32)
        acc_ref[...] = alpha * acc_ref[...] + pv
        return m_new, l_new

    acc_ref[...] = jnp.zeros_like(acc_ref)
    init = (jnp.full((1, tq), NEG_BIG, jnp.float32), jnp.zeros((1, tq), jnp.float32))
    n_far = jnp.maximum(i - 1, 0)
    carry = lax.fori_loop(0, n_far // SPAN, lambda jj, c: update(SPAN * jj, SPAN, c, None), init)
    rem = n_far % SPAN
    tails = [lambda c: update(i, 1, c, "diag")] + [
        functools.partial(lambda c, r: update(i - 1 - r, r + 2, c, "tail"), r=r) for r in range(SPAN)]
    m, l = lax.switch(jnp.where(i == 0, 0, 1 + rem), tails, carry)
    o_ref[...] = (acc_ref[...] / l).T.astype(o_ref.dtype)


def _dsa_attention(big3, mask, bias_near, rel_bias, zero_shape, tq=512):
    B, S, _ = big3.shape
    nq = S // tq
    nc = mask.shape[1]
    assert mask.shape[2] == tq and tq >= REL_MAX_DIST and tq % CHUNK == 0
    hb = HEAD_WIDTH // HEAD_DIM
    steps = B * nq * N_HEADS
    zrows = zero_shape[0] // steps
    assert zrows * steps == zero_shape[0] and zrows % 8 == 0
    step = lambda b, i, h: ((b * nq + i) * N_HEADS + h, 0)
    return pl.pallas_call(
        functools.partial(_dsa_kernel, tq=tq),
        grid=(B, nq, N_HEADS),
        in_specs=[
            pl.BlockSpec(memory_space=pltpu.SMEM),
            pl.BlockSpec((None, tq, HEAD_DIM), lambda b, i, h: (b, i, COL_DQ * hb + h)),
            pl.BlockSpec((None, S, HEAD_DIM), lambda b, i, h: (b, 0, COL_DK * hb + h)),
            pl.BlockSpec((None, S, HEAD_DIM), lambda b, i, h: (b, 0, COL_DV * hb + h)),
            pl.BlockSpec((None, nc, tq, tq), lambda b, i, h: (b, 0, 0, i)),
            pl.BlockSpec((None, 2, tq, tq), lambda b, i, h: (h, 0, 0, 0)),
        ],
        out_specs=[
            pl.BlockSpec((None, tq, HEAD_DIM), lambda b, i, h: (b, i, h)),
            pl.BlockSpec((zrows, zero_shape[1]), step),
        ],
        out_shape=[
            jax.ShapeDtypeStruct((B, S, HEAD_WIDTH), jnp.bfloat16),
            jax.ShapeDtypeStruct(zero_shape, jnp.float32),
        ],
        scratch_shapes=[pltpu.VMEM((HEAD_DIM, tq), jnp.float32)],
        compiler_params=_cparams(("parallel", "parallel", "arbitrary")),
        name="dsa_attn",
    )(rel_bias.T.astype(jnp.float32), big3, big3, big3, mask, bias_near)


N_EXPERTS = 32
EXPERT_TOPK = 4
SWIGLU_ALPHA = 1.702
SWIGLU_LIMIT = 7.0


def _outproj_router_kernel(of_ref, od_ref, x_ref, wof_ref, wod_ref, g_ref, rwh_ref, rwl_ref, rb_ref,
                           x1_ref, xn_ref, eidx_ref, rank_ref, gate_ref, cnt_ref, carry_ref, *, tm):
    i = pl.program_id(0)

    @pl.when(i == 0)
    def _():
        carry_ref[...] = jnp.zeros_like(carry_ref)

    x1 = (x_ref[...]
          + jnp.dot(of_ref[...], wof_ref[...], preferred_element_type=jnp.float32)
          + jnp.dot(od_ref[...], wod_ref[...], preferred_element_type=jnp.float32))
    x1_ref[...] = x1
    ms = jnp.mean(x1 * x1, axis=-1, keepdims=True)
    xn = x1 * lax.rsqrt(ms + NORM_EPS) * g_ref[...]
    xn_ref[...] = xn
    xh = xn.astype(jnp.bfloat16)
    xl = (xn - xh.astype(jnp.float32)).astype(jnp.bfloat16)
    nt = (((1,), (1,)), ((), ()))
    logits = (lax.dot_general(rwh_ref[...], xh, nt, preferred_element_type=jnp.float32)
              + lax.dot_general(rwh_ref[...], xl, nt, preferred_element_type=jnp.float32)
              + lax.dot_general(rwl_ref[...], xh, nt, preferred_element_type=jnp.float32)
              + rb_ref[...])
    eio = lax.broadcasted_iota(jnp.int32, (N_EXPERTS, tm), 0)
    work = logits
    vals, idxs = [], []
    multihot = jnp.zeros((N_EXPERTS, tm), jnp.float32)
    for _ in range(EXPERT_TOPK):
        mx = jnp.max(work, axis=0, keepdims=True)
        ix = jnp.min(jnp.where(work == mx, eio, N_EXPERTS), axis=0, keepdims=True)
        hit = eio == ix
        multihot = jnp.where(hit, 1.0, multihot)
        work = jnp.where(hit, -jnp.inf, work)
        vals.append(mx)
        idxs.append(ix)
    ex = [jnp.exp(v - vals[0]) for v in vals]
    den = ex[0] + ex[1] + ex[2] + ex[3]
    a = lax.broadcasted_iota(jnp.int32, (tm, tm), 0)
    b = lax.broadcasted_iota(jnp.int32, (tm, tm), 1)
    upper = jnp.where(a < b, 1.0, 0.0).astype(jnp.bfloat16)
    before = jnp.dot(multihot.astype(jnp.bfloat16), upper,
                     preferred_element_type=jnp.float32) + carry_ref[:, 0:1]
    for r in range(EXPERT_TOPK):
        eidx_ref[r:r + 1, :] = idxs[r]
        gate_ref[r:r + 1, :] = ex[r] / den
        rank_ref[r:r + 1, :] = jnp.sum(jnp.where(eio == idxs[r], before, 0.0),
                                       axis=0, keepdims=True).astype(jnp.int32)
    carry_ref[...] = carry_ref[...] + jnp.sum(multihot, axis=1, keepdims=True)
    cnt_ref[...] = carry_ref[...].astype(jnp.int32)


def _outproj_router(o_fox, o_dsa, x2, wo_f, wo_d, g, rw_hi, rw_lo, rb, tm=512):
    T, D = x2.shape
    hw = o_fox.shape[1]
    full = lambda shape: pl.BlockSpec(shape, lambda i: tuple(0 for _ in shape))
    return pl.pallas_call(
        functools.partial(_outproj_router_kernel, tm=tm),
        grid=(T // tm,),
        in_specs=[
            pl.BlockSpec((tm, hw), lambda i: (i, 0)),
            pl.BlockSpec((tm, hw), lambda i: (i, 0)),
            pl.BlockSpec((tm, D), lambda i: (i, 0)),
            full((hw, D)), full((hw, D)), full((1, D)),
            full((N_EXPERTS, D)), full((N_EXPERTS, D)), full((N_EXPERTS, 1)),
        ],
        out_specs=[
            pl.BlockSpec((tm, D), lambda i: (i, 0)),
            pl.BlockSpec((tm, D), lambda i: (i, 0)),
            pl.BlockSpec((EXPERT_TOPK, tm), lambda i: (0, i)),
            pl.BlockSpec((EXPERT_TOPK, tm), lambda i: (0, i)),
            pl.BlockSpec((EXPERT_TOPK, tm), lambda i: (0, i)),
            full((N_EXPERTS, 128)),
        ],
        out_shape=[
            jax.ShapeDtypeStruct((T, D), jnp.float32),
            jax.ShapeDtypeStruct((T, D), jnp.float32),
            jax.ShapeDtypeStruct((EXPERT_TOPK, T), jnp.int32),
            jax.ShapeDtypeStruct((EXPERT_TOPK, T), jnp.int32),
            jax.ShapeDtypeStruct((EXPERT_TOPK, T), jnp.float32),
            jax.ShapeDtypeStruct((N_EXPERTS, 128), jnp.int32),
        ],
        scratch_shapes=[pltpu.VMEM((N_EXPERTS, 128), jnp.float32)],
        compiler_params=_cparams(("arbitrary",)),
        name="outproj_router",
    )(o_fox, o_dsa, x2, wo_f, wo_d, g, rw_hi, rw_lo, rb)


ROW_UNROLL = 4


def _for_each_row_copy(tt, fn):
    def body(g, c):
        for u in range(ROW_UNROLL):
            for k in range(EXPERT_TOPK):
                fn(g * ROW_UNROLL + u, k)
        return c

    lax.fori_loop(0, tt // ROW_UNROLL, body, 0)


def _dispatch_kernel(dest_ref, xn_ref, xg_in_ref, xg_ref, sem, *, tt):
    del xg_in_ref

    def row_copy(t, k):
        return pltpu.make_async_copy(xn_ref.at[pl.ds(t, 1)], xg_ref.at[pl.ds(dest_ref[k, t], 1)], sem)

    _for_each_row_copy(tt, lambda t, k: row_copy(t, k).start())
    _for_each_row_copy(tt, lambda t, k: row_copy(t, k).wait())


def _dispatch(dest, xn, xg0, tt=256):
    T, D = xn.shape
    n_rows = xg0.shape[0]
    return pl.pallas_call(
        functools.partial(_dispatch_kernel, tt=tt),
        grid=(T // tt,),
        in_specs=[
            pl.BlockSpec((EXPERT_TOPK, tt), lambda i: (0, i), memory_space=pltpu.SMEM),
            pl.BlockSpec((tt, D), lambda i: (i, 0)),
            pl.BlockSpec(memory_space=pl.ANY),
        ],
        out_specs=pl.BlockSpec(memory_space=pl.ANY),
        out_shape=jax.ShapeDtypeStruct((n_rows, D), xn.dtype),
        scratch_shapes=[pltpu.SemaphoreType.DMA(())],
        input_output_aliases={2: 0},
        compiler_params=_cparams(("arbitrary",)),
        name="moe_dispatch",
    )(dest, xn, xg0)


def _expert_kernel(te_ref, nu_ref, x_ref, w1g_ref, w1l_ref, b1g_ref, b1l_ref, w2_ref, b2_ref,
                   y_ref, xb_ref, acc_ref):
    i = pl.program_id(0)
    f = pl.program_id(1)
    nf = pl.num_programs(1)

    @pl.when(i < nu_ref[0])
    def _():
        @pl.when(f == 0)
        def _():
            xb_ref[...] = x_ref[...].astype(xb_ref.dtype)
            acc_ref[...] = jnp.zeros_like(acc_ref)

        xb = xb_ref[...]
        glu = jnp.dot(xb, w1g_ref[...], preferred_element_type=jnp.float32) + b1g_ref[...]
        lin = jnp.dot(xb, w1l_ref[...], preferred_element_type=jnp.float32) + b1l_ref[...]
        glu = jnp.minimum(glu, SWIGLU_LIMIT)
        lin = jnp.clip(lin, -SWIGLU_LIMIT, SWIGLU_LIMIT)
        act = glu * (1.0 / (1.0 + jnp.exp(-SWIGLU_ALPHA * glu))) * (lin + 1.0)
        acc_ref[...] += jnp.dot(act.astype(jnp.bfloat16), w2_ref[...],
                                preferred_element_type=jnp.float32)

        @pl.when(f == nf - 1)
        def _():
            y_ref[...] = acc_ref[...] + b2_ref[...]

    @pl.when((i >= nu_ref[0]) & (f == nf - 1))
    def _():
        y_ref[...] = jnp.zeros_like(y_ref)


def _experts(tile_expert, n_used, xg, w1g, w1l, b1g, b1l, w2b, b2, tme, tf=1024):
    P, D = xg.shape
    F = w2b.shape[1]
    nf = F // tf
    row = lambda i, f, te, nu: (jnp.minimum(i, nu[0] - 1), 0)
    exp = lambda i, te, nu: te[jnp.minimum(i, nu[0] - 1)]
    fblk = lambda i, f, nu: jnp.where(i < nu[0], f, nf - 1)
    grid_spec = pltpu.PrefetchScalarGridSpec(
        num_scalar_prefetch=2,
        grid=(P // tme, nf),
        in_specs=[
            pl.BlockSpec((tme, D), row),
            pl.BlockSpec((None, D, tf), lambda i, f, te, nu: (exp(i, te, nu), 0, fblk(i, f, nu))),
            pl.BlockSpec((None, D, tf), lambda i, f, te, nu: (exp(i, te, nu), 0, fblk(i, f, nu))),
            pl.BlockSpec((None, 1, tf), lambda i, f, te, nu: (exp(i, te, nu), 0, fblk(i, f, nu))),
            pl.BlockSpec((None, 1, tf), lambda i, f, te, nu: (exp(i, te, nu), 0, fblk(i, f, nu))),
            pl.BlockSpec((None, tf, D), lambda i, f, te, nu: (exp(i, te, nu), fblk(i, f, nu), 0)),
            pl.BlockSpec((None, 1, D), lambda i, f, te, nu: (exp(i, te, nu), 0, 0)),
        ],
        out_specs=pl.BlockSpec((tme, D), lambda i, f, te, nu: (i, 0)),
        scratch_shapes=[pltpu.VMEM((tme, D), jnp.bfloat16), pltpu.VMEM((tme, D), jnp.float32)],
    )
    return pl.pallas_call(
        _expert_kernel,
        grid_spec=grid_spec,
        out_shape=jax.ShapeDtypeStruct((P, D), jnp.float32),
        compiler_params=_cparams(("arbitrary", "arbitrary")),
        name="moe_experts",
    )(tile_expert, n_used, xg, w1g, w1l, b1g, b1l, w2b, b2)


def _combine_kernel(dest_ref, next_dest_ref, x1_ref, gt_ref, y_ref, o_ref, buf_ref, sems, *, tt):
    i = pl.program_id(0)
    n = pl.num_programs(0)
    slot = i % 2

    def row_copy(d_ref, s, t, k):
        return pltpu.make_async_copy(y_ref.at[pl.ds(d_ref[k, t], 1)],
                                     buf_ref.at[s, k, pl.ds(t, 1)], sems.at[s])

    @pl.when(i == 0)
    def _():
        _for_each_row_copy(tt, lambda t, k: row_copy(dest_ref, 0, t, k).start())

    @pl.when(i + 1 < n)
    def _():
        _for_each_row_copy(tt, lambda t, k: row_copy(next_dest_ref, 1 - slot, t, k).start())

    _for_each_row_copy(tt, lambda t, k: row_copy(dest_ref, slot, t, k).wait())
    gt = gt_ref[...]
    out = x1_ref[...]
    for k in range(EXPERT_TOPK):
        out = out + gt[:, k:k + 1] * buf_ref[slot, k]
    o_ref[...] = out


def _combine(dest, x1, gates_t, yrows, tt=128):
    T, D = x1.shape
    n = T // tt
    return pl.pallas_call(
        functools.partial(_combine_kernel, tt=tt),
        grid=(n,),
        in_specs=[
            pl.BlockSpec((EXPERT_TOPK, tt), lambda i: (0, i), memory_space=pltpu.SMEM),
            pl.BlockSpec((EXPERT_TOPK, tt), lambda i: (0, jnp.minimum(i + 1, n - 1)),
                         memory_space=pltpu.SMEM),
            pl.BlockSpec((tt, D), lambda i: (i, 0)),
            pl.BlockSpec((tt, 8), lambda i: (i, 0)),
            pl.BlockSpec(memory_space=pl.ANY),
        ],
        out_specs=pl.BlockSpec((tt, D), lambda i: (i, 0)),
        out_shape=jax.ShapeDtypeStruct((T, D), x1.dtype),
        scratch_shapes=[pltpu.VMEM((2, EXPERT_TOPK, tt, D), jnp.float32),
                        pltpu.SemaphoreType.DMA((2,))],
        compiler_params=_cparams(("arbitrary",)),
        name="moe_combine",
    )(dest, dest, x1, gates_t, yrows)


def kernel_parts(x, attn_norm_g, w_in, fox_gate_b, fox_q_g, fox_k_g, dsa_q_g, dsa_k_g, idx_k_g,
                 rel_bias, w_out, ffn_norm_g, router_w, router_b, w1, b1, w2, b2):
    B, S, D = x.shape
    T = B * S
    l = 0
    f32 = jnp.float32
    wi = w_in[l]
    o = 0
    cols = {}
    for name, width in (("fq", HEAD_WIDTH), ("fk", HEAD_WIDTH), ("fv", HEAD_WIDTH), ("ff", N_HEADS),
                        ("dq", HEAD_WIDTH), ("dk", HEAD_WIDTH), ("dv", HEAD_WIDTH),
                        ("iq", IDX_HEADS * IDX_DIM), ("ik", IDX_DIM), ("iw", IDX_HEADS)):
        cols[name] = wi[:, o:o + width]
        o += width
    w_big = jnp.concatenate([cols[n] for n in ("fq", "fk", "dq", "dk", "fv", "dv", "iq")],
                            axis=1).astype(jnp.bfloat16)
    pad = jnp.zeros((D, SMALL_W - 2 * IDX_DIM - N_HEADS - IDX_HEADS), f32)
    w_small = jnp.concatenate([cols["ik"], cols["ik"], cols["ff"], cols["iw"], pad],
                              axis=1).astype(jnp.bfloat16)
    head_gains = jnp.stack([fox_q_g[l], fox_k_g[l], dsa_q_g[l], dsa_k_g[l]]).reshape(4, 1, HEAD_DIM)
    gate_b128 = jnp.zeros((1, 128), f32).at[0, :N_HEADS].set(fox_gate_b[l])
    ikg128 = jnp.concatenate([idx_k_g[l], idx_k_g[l]]).reshape(1, 128)

    x2 = x.reshape(T, D)
    big, small = _inproj(x2, attn_norm_g[l].reshape(1, D), w_big, w_small, head_gains)
    big3 = big.reshape(B, S, N_BIG_TILES * HEAD_WIDTH)
    small3 = small.reshape(B, S, SMALL_W)
    ccol, crow, ike, iko = _prep(small3, gate_b128, ikg128)
    tq = 512
    ck5 = crow.reshape(B, N_HEADS, S // tq, 1, tq)
    o_fox, w1g, w1l = _fox_attention(big3, ccol, ck5, w1[l], tq=tq)
    topk = min(IDX_TOPK_MAX, S // 4)
    mask, w2b = _indexer_mask(big3, small3, ike, iko, topk, w2[l])
    bias_near = _bias_tiles(rel_bias, tq)
    tme = 512
    n_rows = T * EXPERT_TOPK + N_EXPERTS * tme
    o_dsa, xg0 = _dsa_attention(big3, mask, bias_near, rel_bias, (n_rows, D), tq=tq)
    bf16 = jnp.bfloat16
    wo = w_out[l].astype(bf16)
    rw = router_w[l].T
    rw_hi = rw.astype(bf16)
    rw_lo = (rw - rw_hi.astype(f32)).astype(bf16)
    x1, xn, eidx, rank, gates, cnt = _outproj_router(
        o_fox.reshape(T, HEAD_WIDTH), o_dsa.reshape(T, HEAD_WIDTH), x2, wo[:HEAD_WIDTH], wo[HEAD_WIDTH:],
        ffn_norm_g[l].reshape(1, D), rw_hi, rw_lo, router_b[l].reshape(N_EXPERTS, 1))
    n_tiles = n_rows // tme
    counts = cnt[:, 0]
    tiles_e = (counts + tme - 1) // tme
    tile_end = jnp.cumsum(tiles_e)
    pad_start = (tile_end - tiles_e) * tme
    e_ids = jnp.arange(N_EXPERTS, dtype=jnp.int32)
    dest = rank + jnp.sum(jnp.where(eidx[None] == e_ids[:, None, None], pad_start[:, None, None], 0),
                          axis=0)
    tile_expert = jnp.minimum(
        jnp.sum(tile_end[None, :] <= jnp.arange(n_tiles, dtype=jnp.int32)[:, None], axis=1),
        N_EXPERTS - 1).astype(jnp.int32)
    n_used = tile_end[-1:].astype(jnp.int32)
    xg = _dispatch(dest, xn, xg0)
    F = w2.shape[2]
    b1g = b1[l][:, 0::2].reshape(N_EXPERTS, 1, F)
    b1l = b1[l][:, 1::2].reshape(N_EXPERTS, 1, F)
    yrows = _experts(tile_expert, n_used, xg, w1g, w1l, b1g, b1l, w2b,
                     b2[l].reshape(N_EXPERTS, 1, D), tme)
    gates_t = jnp.pad(gates.T, ((0, 0), (0, 8 - EXPERT_TOPK)))
    out = _combine(dest, x1, gates_t, yrows)
    return dict(o_fox=o_fox, o_dsa=o_dsa, mask=mask, x1=x1, out=out.reshape(B, S, D))


def kernel(x, attn_norm_g, w_in, fox_gate_b, fox_q_g, fox_k_g, dsa_q_g, dsa_k_g, idx_k_g,
           rel_bias, w_out, ffn_norm_g, router_w, router_b, w1, b1, w2, b2):
    return kernel_parts(x, attn_norm_g, w_in, fox_gate_b, fox_q_g, fox_k_g, dsa_q_g, dsa_k_g, idx_k_g,
                        rel_bias, w_out, ffn_norm_g, router_w, router_b, w1, b1, w2, b2)["out"]
```
